```python
import math
import functools
import jax
import jax.numpy as jnp
from jax import lax
import numpy as np

D_MODEL = 1024
BATCH = 32
SEQ = 2048
DEPTH = 1
DEC_BATCH = 128
DEC_SEQ = 4
PAST_LEN = 8192
PAGE_SIZE = 128

MIX_WIDTH = D_MODEL
ATTN_WIDTH = MIX_WIDTH // 2
SSM_WIDTH = MIX_WIDTH - ATTN_WIDTH
HEAD_DIM = 64
N_HEADS = ATTN_WIDTH // HEAD_DIM
ROT_DIM = HEAD_DIM // 4
ROPE_THETA = 500000.0
N_IDX_HEADS = 8
IDX_DIM = 64
TOPK_MAX = 256
Q_BLOCK = 128
SSM_GROUP = 16
N_SSM_GROUPS = SSM_WIDTH // SSM_GROUP
SSM_STATE = 64
DEEPNORM_ALPHA = (2.0 * DEPTH) ** 0.25
DEEPNORM_BETA = (8.0 * DEPTH) ** -0.25
LN_EPS = 1e-5
IN_WIDTH = 4 * ATTN_WIDTH + N_IDX_HEADS * IDX_DIM + IDX_DIM + N_IDX_HEADS + 2 * SSM_WIDTH
F32 = jnp.float32

kernel_name = 'hymba_dsa_s5_deepnorm_step'


def _split_proj(h):
    sizes = [ATTN_WIDTH] * 4 + [N_IDX_HEADS * IDX_DIM, IDX_DIM, N_IDX_HEADS, SSM_WIDTH, SSM_WIDTH]
    offs = np.cumsum(sizes)[:-1].tolist()
    return jnp.split(h, offs, axis=-1)


def _rope_partial(x, pos):
    half = ROT_DIM // 2
    inv = ROPE_THETA ** (-2.0 * jnp.arange(half, dtype=F32) / ROT_DIM)
    ang = pos.astype(F32)[:, None] * inv[None, :]
    cos = jnp.cos(ang)[:, None, :]
    sin = jnp.sin(ang)[:, None, :]
    xr = x[..., :ROT_DIM].astype(F32)
    x1, x2 = xr[..., :half], xr[..., half:]
    rot = jnp.concatenate([x1 * cos - x2 * sin, x2 * cos + x1 * sin], axis=-1)
    return jnp.concatenate([rot.astype(x.dtype), x[..., ROT_DIM:]], axis=-1)


def _project(x, w_in, pos):
    b, t = x.shape[0], x.shape[1]
    h = jnp.einsum('btd,de->bte', x, w_in)
    q, k, v, g_att, qi, ki, wi, u, g_ssm = _split_proj(h)
    q = _rope_partial(q.reshape(b, t, N_HEADS, HEAD_DIM), pos)
    k = _rope_partial(k.reshape(b, t, N_HEADS, HEAD_DIM), pos)
    v = v.reshape(b, t, N_HEADS, HEAD_DIM)
    qi = _rope_partial(qi.reshape(b, t, N_IDX_HEADS, IDX_DIM), pos)
    ki = _rope_partial(ki.reshape(b, t, 1, IDX_DIM), pos).reshape(b, t, IDX_DIM)
    return q, k, v, g_att, qi, ki, wi, u, g_ssm


def _index_topk(qi, wi, ki, qpos, k_sel):
    dots = jnp.einsum('qhd,sd->qhs', qi.astype(F32), ki.astype(F32)) * (IDX_DIM ** -0.5)
    score = jnp.einsum('qhs,qh->qs', jax.nn.relu(dots), wi.astype(F32)) * (N_IDX_HEADS ** -0.5)
    kpos = jnp.arange(ki.shape[0])
    score = jnp.where(kpos[None, :] <= qpos[:, None], score, -jnp.inf)
    _, idx = lax.top_k(score, k_sel)
    return idx


def _sparse_softmax(q, kg, vg, valid):
    s = jnp.einsum('qhd,qkhd->qhk', q.astype(F32), kg.astype(F32)) * (HEAD_DIM ** -0.5)
    s = jnp.where(valid[:, None, :], s, -jnp.inf)
    p = jax.nn.softmax(s, axis=-1)
    return jnp.einsum('qhk,qkhd->qhd', p, vg.astype(F32))


def _prompt_attention_seq(args):
    q, k, v, qi, wi, ki = args
    t = q.shape[0]
    k_sel = min(TOPK_MAX, t // 4)

    def block(i):
        s0 = i * Q_BLOCK
        qb = lax.dynamic_slice_in_dim(q, s0, Q_BLOCK, 0)
        qib = lax.dynamic_slice_in_dim(qi, s0, Q_BLOCK, 0)
        wib = lax.dynamic_slice_in_dim(wi, s0, Q_BLOCK, 0)
        qpos = s0 + jnp.arange(Q_BLOCK)
        idx = _index_topk(qib, wib, ki, qpos, k_sel)
        return _sparse_softmax(qb, k[idx], v[idx], idx <= qpos[:, None])

    out = lax.map(block, jnp.arange(t // Q_BLOCK))
    return out.reshape(t, N_HEADS, HEAD_DIM)


def _sample_attention_seq(q, k_new, v_new, qi, wi, ki_new, pt, cache_k, cache_v, cache_kidx, layer):
    t = q.shape[0]
    n_pages = PAST_LEN // PAGE_SIZE
    k_sel = min(TOPK_MAX, (PAST_LEN + t) // 4)
    ki_past = cache_kidx[layer, pt].reshape(n_pages * PAGE_SIZE, IDX_DIM)
    ki_all = jnp.concatenate([ki_past.astype(ki_new.dtype), ki_new], axis=0)
    qpos = PAST_LEN + jnp.arange(t)
    idx = _index_topk(qi, wi, ki_all, qpos, k_sel)
    in_past = (idx < PAST_LEN)[..., None, None]
    pidx = jnp.minimum(idx, PAST_LEN - 1)
    page = pt[pidx // PAGE_SIZE]
    off = pidx % PAGE_SIZE
    nidx = jnp.clip(idx - PAST_LEN, 0, t - 1)
    kg = jnp.where(in_past, cache_k[layer, page, off].astype(F32), k_new[nidx].astype(F32))
    vg = jnp.where(in_past, cache_v[layer, page, off].astype(F32), v_new[nidx].astype(F32))
    return _sparse_softmax(q, kg, vg, idx <= qpos[:, None])


def _ssm_discretize(a_re, a_im, log_dt, b_re, b_im):
    a_re = a_re.astype(F32)
    a_im = a_im.astype(F32)
    dt = jnp.exp(log_dt.astype(F32))[:, None]
    mag = jnp.exp(dt * a_re)
    abr = mag * jnp.cos(dt * a_im)
    abi = mag * jnp.sin(dt * a_im)
    den = a_re * a_re + a_im * a_im
    nr = abr - 1.0
    cr = (nr * a_re + abi * a_im) / den
    ci = (abi * a_re - nr * a_im) / den
    b_re = b_re.astype(F32)
    b_im = b_im.astype(F32)
    bbr = cr[..., None] * b_re - ci[..., None] * b_im
    bbi = cr[..., None] * b_im + ci[..., None] * b_re
    return abr, abi, bbr, bbi


def _ssm_combine(e1, e2):
    a1r, a1i, b1r, b1i = e1
    a2r, a2i, b2r, b2i = e2
    return (a2r * a1r - a2i * a1i, a2r * a1i + a2i * a1r,
            a2r * b1r - a2i * b1i + b2r, a2r * b1i + a2i * b1r + b2i)


def _ssm_seq(u, h0r, h0i, abr, abi, bbr, bbi, c_re, c_im, d_skip):
    t = u.shape[0]
    ug = u.reshape(t, N_SSM_GROUPS, SSM_GROUP)
    br = jnp.einsum('tgc,gnc->tgn', ug, bbr)
    bi = jnp.einsum('tgc,gnc->tgn', ug, bbi)
    br = br.at[0].add(abr * h0r - abi * h0i)
    bi = bi.at[0].add(abr * h0i + abi * h0r)
    ar = jnp.broadcast_to(abr, br.shape)
    ai = jnp.broadcast_to(abi, bi.shape)
    _, _, hr, hi = lax.associative_scan(_ssm_combine, (ar, ai, br, bi), axis=0)
    y = jnp.einsum('gcn,tgn->tgc', c_re, hr) - jnp.einsum('gcn,tgn->tgc', c_im, hi)
    y = y.reshape(t, SSM_WIDTH) + d_skip * u
    return y, hr[-1], hi[-1]


def _run_ssm(u, h0r, h0i, abr, abi, bbr, bbi, c_re, c_im, d_skip):
    def one(args):
        return _ssm_seq(args[0], args[1], args[2], abr, abi, bbr, bbi, c_re, c_im, d_skip)
    return lax.map(one, (u.astype(F32), h0r.astype(F32), h0i.astype(F32)))


def _finish(x, attn, g_att, y_ssm, g_ssm, w_glu, b_glu, w_out, ln_g, ln_b):
    b, t = x.shape[0], x.shape[1]
    a = attn.reshape(b, t, ATTN_WIDTH) * jax.nn.silu(g_att.astype(F32))
    g = jax.nn.gelu(y_ssm)
    glu = g * jax.nn.sigmoid(jnp.einsum('bte,ef->btf', g, w_glu.astype(F32)) + b_glu.astype(F32))
    s = glu * jax.nn.silu(g_ssm.astype(F32))
    mix = jnp.einsum('bte,ed->btd', jnp.concatenate([a, s], axis=-1), w_out.astype(F32))
    r = DEEPNORM_ALPHA * x.astype(F32) + mix
    mean = jnp.mean(r, axis=-1, keepdims=True)
    var = jnp.mean(jnp.square(r - mean), axis=-1, keepdims=True)
    out = (r - mean) * lax.rsqrt(var + LN_EPS) * ln_g.astype(F32) + ln_b.astype(F32)
    return out.astype(x.dtype)


def setup_inputs(seed: int = 0) -> dict:
    key = jax.random.key(seed)
    ks = jax.random.split(key, 24)
    n_pages = PAST_LEN // PAGE_SIZE
    n_used = DEC_BATCH * n_pages
    n_pool = n_used + n_used // 4
    G, N, C = N_SSM_GROUPS, SSM_STATE, SSM_GROUP
    nrm = jax.random.normal
    x_prompt = nrm(ks[0], (BATCH, SEQ, D_MODEL), F32)
    x_sample = nrm(ks[1], (DEC_BATCH, DEC_SEQ, D_MODEL), F32)
    cache_k = nrm(ks[2], (DEPTH, n_pool, PAGE_SIZE, N_HEADS, HEAD_DIM), F32)
    cache_v = nrm(ks[3], (DEPTH, n_pool, PAGE_SIZE, N_HEADS, HEAD_DIM), F32)
    cache_kidx = nrm(ks[4], (DEPTH, n_pool, PAGE_SIZE, IDX_DIM), F32)
    page_table = jax.random.permutation(ks[5], n_pool)[:n_used].reshape(DEC_BATCH, n_pages).astype(jnp.int32)
    state_ssm_re = 0.3 * nrm(ks[6], (DEPTH, DEC_BATCH, G, N), F32)
    state_ssm_im = 0.3 * nrm(ks[7], (DEPTH, DEC_BATCH, G, N), F32)
    w_in = nrm(ks[8], (DEPTH, D_MODEL, IN_WIDTH), F32) * (D_MODEL ** -0.5)
    w_out = nrm(ks[9], (DEPTH, MIX_WIDTH, D_MODEL), F32) * (MIX_WIDTH ** -0.5) * DEEPNORM_BETA
    ssm_a_re = -0.5 + 0.01 * nrm(ks[10], (DEPTH, G, N), F32)
    ssm_a_im = math.pi * jnp.arange(N, dtype=F32) + 0.01 * nrm(ks[11], (DEPTH, G, N), F32)
    ssm_log_dt = jax.random.uniform(ks[12], (DEPTH, G), F32, math.log(1e-3), math.log(1e-1))
    ssm_b_re = nrm(ks[13], (DEPTH, G, N, C), F32) * ((2.0 * C) ** -0.5)
    ssm_b_im = nrm(ks[14], (DEPTH, G, N, C), F32) * ((2.0 * C) ** -0.5)
    ssm_c_re = nrm(ks[15], (DEPTH, G, C, N), F32) * ((2.0 * N) ** -0.5) * 2.0
    ssm_c_im = nrm(ks[16], (DEPTH, G, C, N), F32) * ((2.0 * N) ** -0.5) * 2.0
    ssm_d = nrm(ks[17], (DEPTH, SSM_WIDTH), F32)
    w_glu = nrm(ks[18], (DEPTH, SSM_WIDTH, SSM_WIDTH), F32) * (SSM_WIDTH ** -0.5)
    b_glu = 0.02 * nrm(ks[19], (DEPTH, SSM_WIDTH), F32)
    ln_g = 1.0 + 0.02 * nrm(ks[20], (DEPTH, D_MODEL), F32)
    ln_b = 0.02 * nrm(ks[21], (DEPTH, D_MODEL), F32)
    return {'x_prompt': x_prompt, 'x_sample': x_sample, 'cache_k': cache_k, 'cache_v': cache_v,
            'cache_kidx': cache_kidx, 'page_table': page_table,
            'state_ssm_re': state_ssm_re, 'state_ssm_im': state_ssm_im,
            'w_in': w_in, 'w_out': w_out, 'ssm_a_re': ssm_a_re, 'ssm_a_im': ssm_a_im,
            'ssm_log_dt': ssm_log_dt, 'ssm_b_re': ssm_b_re, 'ssm_b_im': ssm_b_im,
            'ssm_c_re': ssm_c_re, 'ssm_c_im': ssm_c_im, 'ssm_d': ssm_d,
            'w_glu': w_glu, 'b_glu': b_glu, 'ln_g': ln_g, 'ln_b': ln_b}


def reference(x_prompt, x_sample, cache_k, cache_v, cache_kidx, page_table, state_ssm_re, state_ssm_im,
              w_in, w_out, ssm_a_re, ssm_a_im, ssm_log_dt, ssm_b_re, ssm_b_im, ssm_c_re, ssm_c_im,
              ssm_d, w_glu, b_glu, ln_g, ln_b):
    pos_p = jnp.arange(x_prompt.shape[1])
    pos_s = PAST_LEN + jnp.arange(x_sample.shape[1])
    hp, hs = x_prompt, x_sample
    kp, vp, ip, rp, jp = [], [], [], [], []
    ks_, vs_, is_, rs_, js_ = [], [], [], [], []
    for layer in range(DEPTH):
        abr, abi, bbr, bbi = _ssm_discretize(ssm_a_re[layer], ssm_a_im[layer], ssm_log_dt[layer],
                                             ssm_b_re[layer], ssm_b_im[layer])
        c_re = ssm_c_re[layer].astype(F32)
        c_im = ssm_c_im[layer].astype(F32)
        d_skip = ssm_d[layer].astype(F32)
        q, k, v, g_att, qi, ki, wi, u, g_ssm = _project(hp, w_in[layer], pos_p)
        attn = lax.map(_prompt_attention_seq, (q, k, v, qi, wi, ki))
        h0 = jnp.zeros((hp.shape[0], N_SSM_GROUPS, SSM_STATE), F32)
        y_ssm, hr, hi = _run_ssm(u, h0, h0, abr, abi, bbr, bbi, c_re, c_im, d_skip)
        hp = _finish(hp, attn, g_att, y_ssm, g_ssm, w_glu[layer], b_glu[layer], w_out[layer],
                     ln_g[layer], ln_b[layer])
        kp.append(k); vp.append(v); ip.append(ki); rp.append(hr); jp.append(hi)
        q, k, v, g_att, qi, ki, wi, u, g_ssm = _project(hs, w_in[layer], pos_s)
        attend = jax.vmap(functools.partial(_sample_attention_seq, cache_k=cache_k, cache_v=cache_v,
                                            cache_kidx=cache_kidx, layer=layer))
        attn = attend(q, k, v, qi, wi, ki, page_table)
        y_ssm, hr, hi = _run_ssm(u, state_ssm_re[layer], state_ssm_im[layer], abr, abi, bbr, bbi,
                                 c_re, c_im, d_skip)
        hs = _finish(hs, attn, g_att, y_ssm, g_ssm, w_glu[layer], b_glu[layer], w_out[layer],
                     ln_g[layer], ln_b[layer])
        ks_.append(k); vs_.append(v); is_.append(ki); rs_.append(hr); js_.append(hi)
    y_prompt, y_sample = hp, hs
    new_k_prompt = jnp.stack(kp)
    new_v_prompt = jnp.stack(vp)
    new_kidx_prompt = jnp.stack(ip)
    new_ssm_re_prompt = jnp.stack(rp)
    new_ssm_im_prompt = jnp.stack(jp)
    new_k_sample = jnp.stack(ks_)
    new_v_sample = jnp.stack(vs_)
    new_kidx_sample = jnp.stack(is_)
    new_ssm_re_sample = jnp.stack(rs_)
    new_ssm_im_sample = jnp.stack(js_)
    return (y_prompt, y_sample, new_k_prompt, new_v_prompt, new_kidx_prompt, new_ssm_re_prompt,
            new_ssm_im_prompt, new_k_sample, new_v_sample, new_kidx_sample, new_ssm_re_sample,
            new_ssm_im_sample)
```

```python
import functools
import math

import jax
import jax.numpy as jnp
import numpy as np
from jax import lax
from jax.experimental import pallas as pl
from jax.experimental.pallas import tpu as pltpu

F32 = jnp.float32
BF16 = jnp.bfloat16
I32 = jnp.int32

HEAD_DIM = 64
ROT_DIM = HEAD_DIM // 4
ROPE_THETA = 500000.0
N_IDX_HEADS = 8
IDX_DIM = 64
TOPK_MAX = 256
Q_BLOCK = 128
SSM_GROUP = 16
SSM_STATE = 64
LN_EPS = 1e-5

LANES = 128
SUBLANES = 8
VMEM_LIMIT = 52 * 1024 * 1024

INT_MIN = -(2 ** 31)
NEG_INF = float("-inf")


def _cparams(*sem):
    return pltpu.CompilerParams(dimension_semantics=sem, vmem_limit_bytes=VMEM_LIMIT)


def _ordered_key(score):
    bits = pltpu.bitcast(score, I32)
    return jnp.where(bits < 0, bits ^ jnp.int32(0x7FFFFFFF), bits)


def _count(mask):
    return jnp.sum(jnp.where(mask, 1.0, 0.0), axis=1, keepdims=True)


def _topk_select(key, kpos, k_sel):
    kf = float(k_sel)
    rows = key.shape[0]
    n_idx_bits = max(1, int(math.ceil(math.log2(key.shape[1] + 1))))

    ans0 = jnp.where(_count(key >= 0) >= kf, jnp.int32(0), jnp.int32(INT_MIN))

    def val_step(b, ans):
        cand = ans + jnp.left_shift(jnp.int32(1), jnp.int32(30) - b)
        return jnp.where(_count(key >= cand) >= kf, cand, ans)

    thr = lax.fori_loop(0, 31, val_step, ans0)
    gt = key > thr
    eq = key == thr
    need = kf - _count(gt)
    has_ties = jnp.max(_count(eq) - need) > 0.0

    def tie_search():
        def idx_step(b, j):
            cand = j + jnp.left_shift(jnp.int32(1), jnp.int32(n_idx_bits - 1) - b)
            c = _count(jnp.where(eq, kpos, jnp.int32(2 ** 30)) < cand)
            return jnp.where(c < need, cand, j)
        return lax.fori_loop(0, n_idx_bits, idx_step, jnp.zeros((rows, 1), I32))

    def no_ties():
        return jnp.full((rows, 1), 2 ** 30, I32)

    jthr = lax.cond(has_ties, tie_search, no_ties)
    tie_bias = jnp.where(kpos <= jthr, 0.0, NEG_INF)
    return jnp.where(gt, 0.0, jnp.where(eq, tie_bias, NEG_INF))


def _rope(x, cosf, sin_up, sin_dn):
    half = ROT_DIM // 2
    return (x * cosf + pltpu.roll(x, LANES - half, axis=1) * sin_up
            + pltpu.roll(x, half, axis=1) * sin_dn)


def _proj_kernel(x_ref, w_ref, cos_ref, sup_ref, sdn_ref,
                 q_ref, k_ref, v_ref, ga_ref, qi_ref, kw_ref, u_ref, gs_ref, *, aw, sw):
    xb = x_ref[...].astype(BF16)
    cosf, sup, sdn = cos_ref[...], sup_ref[...], sdn_ref[...]

    def seg(off, width):
        return jnp.dot(xb, w_ref[:, off:off + width], preferred_element_type=F32)

    def rope_seg(h):
        return jnp.concatenate(
            [_rope(h[:, c:c + LANES], cosf, sup, sdn) for c in range(0, h.shape[1], LANES)], axis=1)

    off = 0
    q_ref[...] = rope_seg(seg(off, aw)).astype(BF16); off += aw
    k_ref[...] = rope_seg(seg(off, aw)); off += aw
    v_ref[...] = seg(off, aw); off += aw
    ga_ref[...] = seg(off, aw); off += aw
    qi_ref[...] = rope_seg(seg(off, N_IDX_HEADS * IDX_DIM)).astype(BF16); off += N_IDX_HEADS * IDX_DIM
    kw = seg(off, LANES); off += LANES
    lane = lax.broadcasted_iota(I32, kw.shape, 1)
    is_key = lane < IDX_DIM
    kw_ref[...] = _rope(kw, jnp.where(is_key, cosf, 1.0), jnp.where(is_key, sup, 0.0),
                        jnp.where(is_key, sdn, 0.0))
    u_ref[...] = seg(off, sw); off += sw
    gs_ref[...] = seg(off, sw)


def _rope_tables(pos):
    half = ROT_DIM // 2
    inv = ROPE_THETA ** (-2.0 * jnp.arange(half, dtype=F32) / ROT_DIM)
    ang = pos.astype(F32)[:, None] * inv[None, :]
    cos, sin = jnp.cos(ang), jnp.sin(ang)
    p = pos.shape[0]
    ones = jnp.ones((p, HEAD_DIM - ROT_DIM), F32)
    zeros = jnp.zeros((p, HEAD_DIM - ROT_DIM), F32)
    zh = jnp.zeros((p, half), F32)
    cos64 = jnp.concatenate([cos, cos, ones], axis=1)
    up64 = jnp.concatenate([-sin, zh, zeros], axis=1)
    dn64 = jnp.concatenate([zh, sin, zeros], axis=1)
    rep = LANES // HEAD_DIM
    return (jnp.tile(cos64, (1, rep)), jnp.tile(up64, (1, rep)), jnp.tile(dn64, (1, rep)))


def _project(x2, w_packed, tables, tm, aw, sw):
    rows, d = x2.shape
    n_tab_blocks = tables[0].shape[0] // tm
    row_spec = lambda width: pl.BlockSpec((tm, width), lambda i: (i, 0))
    tab_spec = pl.BlockSpec((tm, LANES), lambda i: (i % n_tab_blocks, 0))
    out_shapes = (
        jax.ShapeDtypeStruct((rows, aw), BF16),
        jax.ShapeDtypeStruct((rows, aw), F32),
        jax.ShapeDtypeStruct((rows, aw), F32),
        jax.ShapeDtypeStruct((rows, aw), F32),
        jax.ShapeDtypeStruct((rows, N_IDX_HEADS * IDX_DIM), BF16),
        jax.ShapeDtypeStruct((rows, LANES), F32),
        jax.ShapeDtypeStruct((rows, sw), F32),
        jax.ShapeDtypeStruct((rows, sw), F32),
    )
    return pl.pallas_call(
        functools.partial(_proj_kernel, aw=aw, sw=sw),
        grid=(rows // tm,),
        in_specs=[row_spec(d), pl.BlockSpec(w_packed.shape, lambda i: (0, 0)),
                  tab_spec, tab_spec, tab_spec],
        out_specs=tuple(row_spec(s.shape[1]) for s in out_shapes),
        out_shape=out_shapes,
        compiler_params=_cparams("arbitrary"),
        name="in_proj",
    )(x2, w_packed, *tables)


def _prompt_attn_kernel(q_ref, qi_ref, kwq_ref, k_ref, v_ref, kw_ref, o_ref,
                        kb_ref, vb_ref, kib_ref, *, n_heads, k_sel, n_groups):
    i = pl.program_id(1)
    t = k_ref.shape[0]

    @pl.when(i == 0)
    def _():
        for h in range(n_heads):
            kb_ref[h] = k_ref[:, h * HEAD_DIM:(h + 1) * HEAD_DIM].astype(BF16)
            vb_ref[h] = v_ref[:, h * HEAD_DIM:(h + 1) * HEAD_DIM].astype(BF16)
        kib_ref[...] = kw_ref[:, :IDX_DIM].astype(BF16)

    nt = (((1,), (1,)), ((), ()))
    w_scale = (IDX_DIM ** -0.5) * (N_IDX_HEADS ** -0.5)

    def body(s):
        qi = qi_ref[...]
        wq = kwq_ref[:, IDX_DIM:IDX_DIM + N_IDX_HEADS] * w_scale
        kib = kib_ref[0:s, :]
        score = jnp.zeros((Q_BLOCK, s), F32)
        for h in range(N_IDX_HEADS):
            d = lax.dot_general(qi[:, h * IDX_DIM:(h + 1) * IDX_DIM], kib, nt,
                                preferred_element_type=F32)
            score = score + jnp.maximum(d, 0.0) * wq[:, h:h + 1]
        kpos = lax.broadcasted_iota(I32, (Q_BLOCK, s), 1)
        qpos = i * Q_BLOCK + lax.broadcasted_iota(I32, (Q_BLOCK, s), 0)
        valid = kpos <= qpos
        key = jnp.where(valid, _ordered_key(score), jnp.int32(INT_MIN))
        sel = _topk_select(key, kpos, k_sel)
        bias = jnp.where(valid, sel, NEG_INF)

        q = q_ref[...]
        for h in range(n_heads):
            sc = lax.dot_general(q[:, h * HEAD_DIM:(h + 1) * HEAD_DIM], kb_ref[h, 0:s, :], nt,
                                 preferred_element_type=F32) * (HEAD_DIM ** -0.5) + bias
            m = jnp.max(sc, axis=1, keepdims=True)
            p = jnp.exp(sc - m)
            l = jnp.sum(p, axis=1, keepdims=True)
            o = jnp.dot(p.astype(BF16), vb_ref[h, 0:s, :], preferred_element_type=F32)
            o_ref[:, h * HEAD_DIM:(h + 1) * HEAD_DIM] = o / l

    blocks_per_group = (t // Q_BLOCK) // n_groups
    for g in range(n_groups):
        pl.when(i // blocks_per_group == g)(
            functools.partial(body, (g + 1) * blocks_per_group * Q_BLOCK))


def _prompt_attention(q, qi, kw, k, v, batch, t, n_heads, k_sel):
    aw = n_heads * HEAD_DIM
    nq = t // Q_BLOCK
    n_groups = 4 if nq % 4 == 0 else 1
    qspec = lambda width: pl.BlockSpec((Q_BLOCK, width), lambda b, i: (b * nq + i, 0))
    sspec = lambda width: pl.BlockSpec((t, width), lambda b, i: (b, 0))
    return pl.pallas_call(
        functools.partial(_prompt_attn_kernel, n_heads=n_heads, k_sel=k_sel, n_groups=n_groups),
        grid=(batch, nq),
        in_specs=[qspec(aw), qspec(N_IDX_HEADS * IDX_DIM), qspec(LANES),
                  sspec(aw), sspec(aw), sspec(LANES)],
        out_specs=qspec(aw),
        out_shape=jax.ShapeDtypeStruct((batch * t, aw), F32),
        scratch_shapes=[pltpu.VMEM((n_heads, t, HEAD_DIM), BF16),
                        pltpu.VMEM((n_heads, t, HEAD_DIM), BF16),
                        pltpu.VMEM((t, IDX_DIM), BF16)],
        compiler_params=_cparams("arbitrary", "arbitrary"),
        name="prompt_attn",
    )(q, qi, kw, k, v, kw)


SCAN_LANES = 512


def _ssm_prompt_kernel(u_ref, h0r_ref, h0i_ref, bwr_ref, bwi_ref, cwr_ref, cwi_ref, d_ref, tab_ref,
                       y_ref, hr_ref, hi_ref, br_scr, bi_scr, cr_scr, ci_scr):
    c = pl.program_id(1)
    tc = u_ref.shape[0]
    gn = br_scr.shape[1]

    @pl.when(c == 0)
    def _():
        cr_scr[...] = h0r_ref[0]
        ci_scr[...] = h0i_ref[0]

    u = u_ref[...]
    ub = u.astype(BF16)
    n_in = bwr_ref.shape[0]
    kw = bwr_ref.shape[1]
    nw = bwr_ref.shape[2]
    for j in range(n_in):
        uj = ub[:, j * kw:(j + 1) * kw]
        br_scr[:, j * nw:(j + 1) * nw] = jnp.dot(uj, bwr_ref[j], preferred_element_type=F32)
        bi_scr[:, j * nw:(j + 1) * nw] = jnp.dot(uj, bwi_ref[j], preferred_element_type=F32)

    for lo in range(0, gn, SCAN_LANES):
        sl = slice(lo, lo + SCAN_LANES)
        a1r, a1i = tab_ref[0, :, sl], tab_ref[1, :, sl]
        a2r, a2i = tab_ref[2, :, sl], tab_ref[3, :, sl]
        a4r, a4i = tab_ref[4, :, sl], tab_ref[5, :, sl]
        pr, pi = tab_ref[6, :, sl], tab_ref[7, :, sl]

        def tile(j, carry):
            cr, ci = carry
            rows = pl.ds(pl.multiple_of(j * SUBLANES, SUBLANES), SUBLANES)
            br, bi = br_scr[rows, sl], bi_scr[rows, sl]
            for shift, ar, ai in ((1, a1r, a1i), (2, a2r, a2i), (4, a4r, a4i)):
                sr = pltpu.roll(br, shift, axis=0)
                si = pltpu.roll(bi, shift, axis=0)
                br, bi = br + (ar * sr - ai * si), bi + (ar * si + ai * sr)
            hr = br + (pr * cr - pi * ci)
            hi = bi + (pr * ci + pi * cr)
            br_scr[rows, sl] = hr
            bi_scr[rows, sl] = hi
            return hr[SUBLANES - 1:SUBLANES, :], hi[SUBLANES - 1:SUBLANES, :]

        cr, ci = lax.fori_loop(0, tc // SUBLANES, tile, (cr_scr[:, sl], ci_scr[:, sl]))
        cr_scr[:, sl] = cr
        ci_scr[:, sl] = ci

    y = (jnp.dot(br_scr[...].astype(BF16), cwr_ref[...], preferred_element_type=F32)
         - jnp.dot(bi_scr[...].astype(BF16), cwi_ref[...], preferred_element_type=F32))
    y_ref[...] = y + d_ref[...] * u
    hr_ref[0] = cr_scr[...]
    hi_ref[0] = ci_scr[...]


def _ssm_prompt(u, h0r, h0i, ssm_w, batch, t, tc):
    bwr, bwi, cwr, cwi, d_skip, tab = ssm_w
    sw = u.shape[1]
    gn = cwr.shape[0]
    nchunk = t // tc
    full = lambda a: pl.BlockSpec(a.shape, lambda b, c: (0,) * a.ndim)
    st_spec = pl.BlockSpec((1, 1, gn), lambda b, c: (b, 0, 0))
    u_spec = pl.BlockSpec((tc, sw), lambda b, c: (b * nchunk + c, 0))
    return pl.pallas_call(
        _ssm_prompt_kernel,
        grid=(batch, nchunk),
        in_specs=[u_spec, st_spec, st_spec, full(bwr), full(bwi), full(cwr), full(cwi),
                  full(d_skip), full(tab)],
        out_specs=(u_spec, st_spec, st_spec),
        out_shape=(jax.ShapeDtypeStruct((batch * t, sw), F32),
                   jax.ShapeDtypeStruct((batch, 1, gn), F32),
                   jax.ShapeDtypeStruct((batch, 1, gn), F32)),
        scratch_shapes=[pltpu.VMEM((tc, gn), F32), pltpu.VMEM((tc, gn), F32),
                        pltpu.VMEM((1, gn), F32), pltpu.VMEM((1, gn), F32)],
        compiler_params=_cparams("arbitrary", "arbitrary"),
        name="ssm_prompt",
    )(u, h0r, h0i, bwr, bwi, cwr, cwi, d_skip, tab)


def _ssm_sample_kernel(u_ref, h0r_ref, h0i_ref, bwr_ref, bwi_ref, cwr_ref, cwi_ref, d_ref, tab_ref,
                       y_ref, hr_ref, hi_ref, *, steps):
    sw = d_ref.shape[1]
    n_in, kw, nw = bwr_ref.shape
    ar, ai = tab_ref[6, 0:1, :], tab_ref[7, 0:1, :]
    hr, hi = h0r_ref[...], h0i_ref[...]
    for s in range(steps):
        u = u_ref[:, s * sw:(s + 1) * sw]
        ub = u.astype(BF16)
        br = jnp.concatenate([jnp.dot(ub[:, j * kw:(j + 1) * kw], bwr_ref[j], preferred_element_type=F32)
                              for j in range(n_in)], axis=1)
        bi = jnp.concatenate([jnp.dot(ub[:, j * kw:(j + 1) * kw], bwi_ref[j], preferred_element_type=F32)
                              for j in range(n_in)], axis=1)
        hr, hi = (ar * hr - ai * hi) + br, (ar * hi + ai * hr) + bi
        y = (jnp.dot(hr.astype(BF16), cwr_ref[...], preferred_element_type=F32)
             - jnp.dot(hi.astype(BF16), cwi_ref[...], preferred_element_type=F32))
        y_ref[:, s * sw:(s + 1) * sw] = y + d_ref[...] * u
    hr_ref[...] = hr
    hi_ref[...] = hi


def _ssm_sample(u2, h0r, h0i, ssm_w, steps):
    bwr, bwi, cwr, cwi, d_skip, tab = ssm_w
    nseq = u2.shape[0]
    gn = cwr.shape[0]
    return pl.pallas_call(
        functools.partial(_ssm_sample_kernel, steps=steps),
        out_shape=(jax.ShapeDtypeStruct(u2.shape, F32),
                   jax.ShapeDtypeStruct((nseq, gn), F32),
                   jax.ShapeDtypeStruct((nseq, gn), F32)),
        compiler_params=pltpu.CompilerParams(vmem_limit_bytes=VMEM_LIMIT),
        name="ssm_sample",
    )(u2, h0r, h0i, bwr, bwi, cwr, cwi, d_skip, tab)


def _finish_kernel(x_ref, at_ref, ga_ref, ys_ref, gs_ref, wglu_ref, bglu_ref, woa_ref, wos_ref,
                   lng_ref, lnb_ref, o_ref, *, alpha):
    a = at_ref[...] * jax.nn.silu(ga_ref[...])
    g = jax.nn.gelu(ys_ref[...], approximate=True)
    z = jnp.dot(g.astype(BF16), wglu_ref[...], preferred_element_type=F32) + bglu_ref[...]
    s = g * jax.nn.sigmoid(z) * jax.nn.silu(gs_ref[...])
    mix = (jnp.dot(a.astype(BF16), woa_ref[...], preferred_element_type=F32)
           + jnp.dot(s.astype(BF16), wos_ref[...], preferred_element_type=F32))
    r = alpha * x_ref[...] + mix
    mean = jnp.mean(r, axis=-1, keepdims=True)
    cen = r - mean
    var = jnp.mean(cen * cen, axis=-1, keepdims=True)
    o_ref[...] = cen * lax.rsqrt(var + LN_EPS) * lng_ref[...] + lnb_ref[...]


def _finish(x2, attn, g_att, y_ssm, g_ssm, fin_w, tm, alpha):
    rows, d = x2.shape
    row_spec = lambda a: pl.BlockSpec((tm, a.shape[1]), lambda i: (i, 0))
    full = lambda a: pl.BlockSpec(a.shape, lambda i: (0,) * a.ndim)
    acts = (x2, attn, g_att, y_ssm, g_ssm)
    return pl.pallas_call(
        functools.partial(_finish_kernel, alpha=alpha),
        grid=(rows // tm,),
        in_specs=[row_spec(a) for a in acts] + [full(w) for w in fin_w],
        out_specs=pl.BlockSpec((tm, d), lambda i: (i, 0)),
        out_shape=jax.ShapeDtypeStruct((rows, d), F32),
        compiler_params=_cparams("arbitrary"),
        name="finish",
    )(*acts, *fin_w)


def _page_copies(pt_ref, src_hbm, buf, sem, seq, first_page, n, slot):
    def copy(j):
        return pltpu.make_async_copy(src_hbm.at[pt_ref[seq, first_page + j]], buf.at[slot, j],
                                     sem.at[slot])
    return copy


def _start_pages(pt_ref, src_hbm, buf, sem, seq, first_page, n, slot):
    copy = _page_copies(pt_ref, src_hbm, buf, sem, seq, first_page, n, slot)

    def go(j, carry):
        copy(j).start()
        return carry
    lax.fori_loop(0, n, go, 0)


def _wait_pages(pt_ref, src_hbm, buf, sem, seq, first_page, n, slot):
    copy = _page_copies(pt_ref, src_hbm, buf, sem, seq, first_page, n, slot)

    def go(j, carry):
        copy(j).wait()
        return carry
    lax.fori_loop(0, n, go, 0)


def _sample_index_kernel(pt_ref, qi_ref, wi_ref, kin_ref, kidx_hbm, bias_ref, buf, sem,
                         *, n_pages, page, steps, k_sel):
    s = pl.program_id(0)
    nseq = pl.num_programs(0)
    slot = s % 2
    args = (pt_ref, kidx_hbm, buf, sem)

    @pl.when(s == 0)
    def _():
        _start_pages(*args, 0, 0, n_pages, 0)

    @pl.when(s + 1 < nseq)
    def _():
        _start_pages(*args, s + 1, 0, n_pages, 1 - slot)

    _wait_pages(*args, s, 0, n_pages, slot)

    past = n_pages * page
    nt = (((1,), (1,)), ((), ()))
    qi = qi_ref[0]
    wi = wi_ref[0] * ((IDX_DIM ** -0.5) * (N_IDX_HEADS ** -0.5))
    ki_past = buf[slot].reshape(past, IDX_DIM).astype(BF16)
    ki_new = jnp.concatenate(
        [kin_ref[0][:, :IDX_DIM], jnp.zeros((LANES - SUBLANES, IDX_DIM), F32)], axis=0).astype(BF16)

    def head_sum(d):
        w = jnp.maximum(d, 0.0) * wi
        out = w[0:SUBLANES]
        for h in range(1, N_IDX_HEADS):
            out = out + w[h * SUBLANES:(h + 1) * SUBLANES]
        return out

    sc_past = head_sum(lax.dot_general(qi, ki_past, nt, preferred_element_type=F32))
    sc_new = head_sum(lax.dot_general(qi, ki_new, nt, preferred_element_type=F32))
    score = jnp.concatenate([sc_past, sc_new], axis=1)
    shape = score.shape
    kpos = lax.broadcasted_iota(I32, shape, 1)
    qpos = past + lax.broadcasted_iota(I32, shape, 0)
    valid = kpos <= jnp.minimum(qpos, past + steps - 1)
    key = jnp.where(valid, _ordered_key(score), jnp.int32(INT_MIN))
    sel = _topk_select(key, kpos, k_sel)
    bias_ref[0] = jnp.where(valid, sel, NEG_INF)


def _sample_index(page_table, qi_hs, wi_hs, kw_new, kidx_pages, steps, k_sel):
    nseq, n_pages = page_table.shape
    page = kidx_pages.shape[1]
    past = n_pages * page
    grid_spec = pltpu.PrefetchScalarGridSpec(
        num_scalar_prefetch=1,
        grid=(nseq,),
        in_specs=[pl.BlockSpec((1,) + qi_hs.shape[1:], lambda s, pt: (s, 0, 0)),
                  pl.BlockSpec((1,) + wi_hs.shape[1:], lambda s, pt: (s, 0, 0)),
                  pl.BlockSpec((1,) + kw_new.shape[1:], lambda s, pt: (s, 0, 0)),
                  pl.BlockSpec(memory_space=pl.ANY)],
        out_specs=pl.BlockSpec((1, SUBLANES, past + LANES), lambda s, pt: (s, 0, 0)),
        scratch_shapes=[pltpu.VMEM((2, n_pages, page, IDX_DIM), F32),
                        pltpu.SemaphoreType.DMA((2,))],
    )
    return pl.pallas_call(
        functools.partial(_sample_index_kernel, n_pages=n_pages, page=page, steps=steps, k_sel=k_sel),
        grid_spec=grid_spec,
        out_shape=jax.ShapeDtypeStruct((nseq, SUBLANES, past + LANES), F32),
        compiler_params=_cparams("arbitrary"),
        name="sample_index",
    )(page_table, qi_hs, wi_hs, kw_new, kidx_pages)


def _sample_attn_kernel(pt_ref, q_ref, bias_ref, biasn_ref, kn_ref, vn_ref, k_hbm, v_hbm, o_ref,
                        kbuf, vbuf, ksem, vsem, m_scr, l_scr, acc_scr,
                        *, n_chunks, cpages, page, n_heads):
    step = pl.program_id(0)
    nsteps = pl.num_programs(0)
    slot = step % 2
    aw = n_heads * HEAD_DIM

    def start(item, sl):
        seq, ch = item // n_chunks, item % n_chunks
        _start_pages(pt_ref, k_hbm, kbuf, ksem, seq, ch * cpages, cpages, sl)
        _start_pages(pt_ref, v_hbm, vbuf, vsem, seq, ch * cpages, cpages, sl)

    @pl.when(step == 0)
    def _():
        start(step, 0)

    @pl.when(step + 1 < nsteps)
    def _():
        start(step + 1, 1 - slot)

    seq, ch = step // n_chunks, step % n_chunks
    _wait_pages(pt_ref, k_hbm, kbuf, ksem, seq, ch * cpages, cpages, slot)
    _wait_pages(pt_ref, v_hbm, vbuf, vsem, seq, ch * cpages, cpages, slot)

    @pl.when(ch == 0)
    def _():
        m_scr[...] = jnp.full(m_scr.shape, NEG_INF, F32)
        l_scr[...] = jnp.zeros(l_scr.shape, F32)
        acc_scr[...] = jnp.zeros(acc_scr.shape, F32)

    nt = (((1,), (1,)), ((), ()))
    q = q_ref[0]

    def update(kc, vc, bias8):
        sc = lax.dot_general(q, kc, nt, preferred_element_type=F32) * (HEAD_DIM ** -0.5)
        sc = sc + jnp.concatenate([bias8] * n_heads, axis=0)
        m_old = m_scr[...]
        m_new = jnp.maximum(m_old, jnp.max(sc, axis=1, keepdims=True))
        m_safe = jnp.where(m_new == NEG_INF, 0.0, m_new)
        alpha = jnp.exp(m_old - m_safe)
        p = jnp.exp(sc - m_safe)
        l_scr[...] = alpha * l_scr[...] + jnp.sum(p, axis=1, keepdims=True)
        acc_scr[...] = alpha * acc_scr[...] + jnp.dot(p.astype(BF16), vc, preferred_element_type=F32)
        m_scr[...] = m_new

    cs = cpages * page
    update(kbuf[slot].reshape(cs, aw).astype(BF16), vbuf[slot].reshape(cs, aw).astype(BF16),
           bias_ref[0])

    @pl.when(ch == n_chunks - 1)
    def _():
        pad = jnp.zeros((LANES - SUBLANES, aw), F32)
        update(jnp.concatenate([kn_ref[0], pad], axis=0).astype(BF16),
               jnp.concatenate([vn_ref[0], pad], axis=0).astype(BF16), biasn_ref[0])
        o = acc_scr[...] / l_scr[...]
        row_h = lax.broadcasted_iota(I32, o.shape, 0) // SUBLANES
        lane_h = lax.broadcasted_iota(I32, o.shape, 1) // HEAD_DIM
        o = jnp.where(row_h == lane_h, o, 0.0)
        out = o[0:SUBLANES]
        for h in range(1, n_heads):
            out = out + o[h * SUBLANES:(h + 1) * SUBLANES]
        o_ref[0] = out


def _sample_attention(page_table, q_bd, bias, k_new, v_new, k_pages, v_pages, n_heads, cpages):
    nseq, n_pages = page_table.shape
    page, aw = k_pages.shape[1], k_pages.shape[2]
    n_chunks = n_pages // cpages
    cs = cpages * page
    rows = n_heads * SUBLANES
    grid_spec = pltpu.PrefetchScalarGridSpec(
        num_scalar_prefetch=1,
        grid=(nseq * n_chunks,),
        in_specs=[pl.BlockSpec((1, rows, aw), lambda i, pt: (i // n_chunks, 0, 0)),
                  pl.BlockSpec((1, SUBLANES, cs), lambda i, pt: (i // n_chunks, 0, i % n_chunks)),
                  pl.BlockSpec((1, SUBLANES, LANES), lambda i, pt: (i // n_chunks, 0, n_pages * page // LANES)),
                  pl.BlockSpec((1, SUBLANES, aw), lambda i, pt: (i // n_chunks, 0, 0)),
                  pl.BlockSpec((1, SUBLANES, aw), lambda i, pt: (i // n_chunks, 0, 0)),
                  pl.BlockSpec(memory_space=pl.ANY),
                  pl.BlockSpec(memory_space=pl.ANY)],
        out_specs=pl.BlockSpec((1, SUBLANES, aw), lambda i, pt: (i // n_chunks, 0, 0)),
        scratch_shapes=[pltpu.VMEM((2, cpages, page, aw), F32),
                        pltpu.VMEM((2, cpages, page, aw), F32),
                        pltpu.SemaphoreType.DMA((2,)),
                        pltpu.SemaphoreType.DMA((2,)),
                        pltpu.VMEM((rows, 1), F32),
                        pltpu.VMEM((rows, 1), F32),
                        pltpu.VMEM((rows, aw), F32)],
    )
    return pl.pallas_call(
        functools.partial(_sample_attn_kernel, n_chunks=n_chunks, cpages=cpages, page=page,
                          n_heads=n_heads),
        grid_spec=grid_spec,
        out_shape=jax.ShapeDtypeStruct((nseq, SUBLANES, aw), F32),
        compiler_params=_cparams("arbitrary"),
        name="sample_attn",
    )(page_table, q_bd, bias, bias, k_new, v_new, k_pages, v_pages)


def _pack_w_in(w, aw, sw):
    sizes = [aw] * 4 + [N_IDX_HEADS * IDX_DIM, IDX_DIM, N_IDX_HEADS, sw, sw]
    offs = np.cumsum([0] + sizes)
    seg = [w[:, offs[j]:offs[j + 1]] for j in range(len(sizes))]
    pad = jnp.zeros((w.shape[0], LANES - IDX_DIM - N_IDX_HEADS), w.dtype)
    return jnp.concatenate(seg[:5] + [seg[5], seg[6], pad] + seg[7:], axis=1).astype(BF16)


def _ssm_weights(a_re, a_im, log_dt, b_re, b_im, c_re, c_im, d_skip):
    g, n, c = b_re.shape
    a_re, a_im = a_re.astype(F32), a_im.astype(F32)
    dt = jnp.exp(log_dt.astype(F32))[:, None]
    mag = jnp.exp(dt * a_re)
    abr = mag * jnp.cos(dt * a_im)
    abi = mag * jnp.sin(dt * a_im)
    den = a_re * a_re + a_im * a_im
    nr = abr - 1.0
    cr = (nr * a_re + abi * a_im) / den
    ci = (abi * a_re - nr * a_im) / den
    b_re, b_im = b_re.astype(F32), b_im.astype(F32)
    bbr = cr[..., None] * b_re - ci[..., None] * b_im
    bbi = cr[..., None] * b_im + ci[..., None] * b_re

    kw_lanes = 2 * LANES
    gpk = kw_lanes // c
    n_in = g // gpk
    eye = jnp.eye(gpk, dtype=F32)

    def in_map(bb):
        bb = bb.reshape(n_in, gpk, n, c)
        return jnp.einsum("jgnc,gh->jgchn", bb, eye).reshape(n_in, gpk * c, gpk * n).astype(BF16)

    def out_map(cc):
        return jnp.einsum("gcn,gh->gnhc", cc.astype(F32), jnp.eye(g, dtype=F32)).reshape(g * n, g * c).astype(BF16)

    ar, ai = abr.reshape(1, g * n), abi.reshape(1, g * n)
    pows_r, pows_i = [ar], [ai]
    for _ in range(SUBLANES - 1):
        pr, pi = pows_r[-1], pows_i[-1]
        pows_r.append(pr * ar - pi * ai)
        pows_i.append(pr * ai + pi * ar)
    row = jnp.arange(SUBLANES)[:, None]

    def masked(k):
        keep = row >= k
        return (jnp.where(keep, pows_r[k - 1], 0.0), jnp.where(keep, pows_i[k - 1], 0.0))

    m1, m2, m4 = masked(1), masked(2), masked(4)
    tab = jnp.stack([m1[0], m1[1], m2[0], m2[1], m4[0], m4[1],
                     jnp.concatenate(pows_r, axis=0), jnp.concatenate(pows_i, axis=0)], axis=0)
    return (in_map(bbr), in_map(bbi), out_map(c_re), out_map(c_im),
            d_skip.astype(F32).reshape(1, g * c), tab)


def kernel(x_prompt, x_sample, cache_k, cache_v, cache_kidx, page_table, state_ssm_re, state_ssm_im,
           w_in, w_out, ssm_a_re, ssm_a_im, ssm_log_dt, ssm_b_re, ssm_b_im, ssm_c_re, ssm_c_im,
           ssm_d, w_glu, b_glu, ln_g, ln_b):
    depth = w_in.shape[0]
    assert depth == 1, "single-layer trunk"
    batch, t, d = x_prompt.shape
    nseq, steps, _ = x_sample.shape
    n_pool, page, n_heads, head_dim = cache_k.shape[1:]
    assert head_dim == HEAD_DIM and steps <= SUBLANES
    aw = n_heads * HEAD_DIM
    g_ssm_n = ssm_b_re.shape[1]
    sw = g_ssm_n * SSM_GROUP
    gn = g_ssm_n * SSM_STATE
    n_pages = page_table.shape[1]
    past = n_pages * page
    alpha = (2.0 * depth) ** 0.25

    layer = 0
    w_packed = _pack_w_in(w_in[layer], aw, sw)
    ssm_w = _ssm_weights(ssm_a_re[layer], ssm_a_im[layer], ssm_log_dt[layer], ssm_b_re[layer],
                         ssm_b_im[layer], ssm_c_re[layer], ssm_c_im[layer], ssm_d[layer])
    fin_w = (w_glu[layer].astype(BF16), b_glu[layer].astype(F32).reshape(1, sw),
             w_out[layer][:aw].astype(BF16), w_out[layer][aw:].astype(BF16),
             ln_g[layer].astype(F32).reshape(1, d), ln_b[layer].astype(F32).reshape(1, d))

    tm = 512
    xp2 = x_prompt.reshape(batch * t, d)
    q, k, v, g_att, qi, kw, u, g_ssm = _project(xp2, w_packed, _rope_tables(jnp.arange(t)), tm, aw, sw)
    attn = _prompt_attention(q, qi, kw, k, v, batch, t, n_heads, min(TOPK_MAX, t // 4))
    zeros_state = jnp.zeros((batch, 1, gn), F32)
    y_ssm, hr_p, hi_p = _ssm_prompt(u, zeros_state, zeros_state, ssm_w, batch, t, min(128, t))
    y_prompt = _finish(xp2, attn, g_att, y_ssm, g_ssm, fin_w, tm, alpha).reshape(batch, t, d)
    new_k_p = k.reshape(1, batch, t, n_heads, HEAD_DIM)
    new_v_p = v.reshape(1, batch, t, n_heads, HEAD_DIM)
    new_ki_p = kw[:, :IDX_DIM].reshape(1, batch, t, IDX_DIM)
    new_hr_p = hr_p.reshape(1, batch, g_ssm_n, SSM_STATE)
    new_hi_p = hi_p.reshape(1, batch, g_ssm_n, SSM_STATE)

    rows_s = nseq * steps
    xs2 = x_sample.reshape(rows_s, d)
    pos_s = past + (jnp.arange(rows_s) % steps)
    qs, ks, vs, gas, qis, kws, us, gss = _project(xs2, w_packed, _rope_tables(pos_s), rows_s, aw, sw)

    def head_rows(a, width):
        a = a.reshape(nseq, steps, -1, width).transpose(0, 2, 1, 3)
        return jnp.pad(a, ((0, 0), (0, 0), (0, SUBLANES - steps), (0, 0)))

    def step_rows(a):
        return jnp.pad(a.reshape(nseq, steps, -1), ((0, 0), (0, SUBLANES - steps), (0, 0)))

    qi_hs = head_rows(qis, IDX_DIM).reshape(nseq, N_IDX_HEADS * SUBLANES, IDX_DIM)
    wi_hs = head_rows(kws[:, IDX_DIM:IDX_DIM + N_IDX_HEADS], 1).reshape(nseq, N_IDX_HEADS * SUBLANES, 1)
    q_hs = head_rows(qs, HEAD_DIM)
    q_bd = jnp.einsum("shtd,hg->shtgd", q_hs, jnp.eye(n_heads, dtype=q_hs.dtype))
    q_bd = q_bd.reshape(nseq, n_heads * SUBLANES, aw)

    k_sel_s = min(TOPK_MAX, (past + steps) // 4)
    bias = _sample_index(page_table, qi_hs, wi_hs, step_rows(kws),
                         cache_kidx[layer].reshape(n_pool, page, IDX_DIM), steps, k_sel_s)
    cpages = 16 if n_pages % 16 == 0 else n_pages
    attn_s = _sample_attention(page_table, q_bd, bias, step_rows(ks), step_rows(vs),
                               cache_k[layer].reshape(n_pool, page, aw),
                               cache_v[layer].reshape(n_pool, page, aw), n_heads, cpages)
    attn_s = attn_s[:, :steps].reshape(rows_s, aw)

    y_s, hr_s, hi_s = _ssm_sample(us.reshape(nseq, steps * sw),
                                  state_ssm_re[layer].reshape(nseq, gn).astype(F32),
                                  state_ssm_im[layer].reshape(nseq, gn).astype(F32), ssm_w, steps)
    y_sample = _finish(xs2, attn_s, gas, y_s.reshape(rows_s, sw), gss, fin_w, rows_s, alpha)
    y_sample = y_sample.reshape(nseq, steps, d)

    return (y_prompt, y_sample, new_k_p, new_v_p, new_ki_p, new_hr_p, new_hi_p,
            ks.reshape(1, nseq, steps, n_heads, HEAD_DIM), vs.reshape(1, nseq, steps, n_heads, HEAD_DIM),
            kws[:, :IDX_DIM].reshape(1, nseq, steps, IDX_DIM),
            hr_s.reshape(1, nseq, g_ssm_n, SSM_STATE), hi_s.reshape(1, nseq, g_ssm_n, SSM_STATE))
```

```python
import functools
import math

import jax
import jax.numpy as jnp
import numpy as np
from jax import lax
from jax.experimental import pallas as pl
from jax.experimental.pallas import tpu as pltpu

F32 = jnp.float32
BF16 = jnp.bfloat16
I32 = jnp.int32

HEAD_DIM = 64
ROT_DIM = HEAD_DIM // 4
ROPE_THETA = 500000.0
N_IDX_HEADS = 8
IDX_DIM = 64
TOPK_MAX = 256
Q_BLOCK = 128
SSM_GROUP = 16
SSM_STATE = 64
LN_EPS = 1e-5

LANES = 128
SUBLANES = 8
VMEM_LIMIT = 52 * 1024 * 1024

INT_MIN = -(2 ** 31)
NEG_INF = float("-inf")


def _cparams(*sem):
    return pltpu.CompilerParams(dimension_semantics=sem, vmem_limit_bytes=VMEM_LIMIT)


def _ordered_key(score):
    bits = pltpu.bitcast(score, I32)
    return jnp.where(bits < 0, bits ^ jnp.int32(0x7FFFFFFF), bits)


REDUCE_CHAINS = 8


def _reduce(x, kind, axis):
    pair, full = (jnp.add, jnp.sum) if kind == "sum" else (jnp.maximum, jnp.max)
    n = x.shape[axis]
    unit = SUBLANES if axis == 0 else LANES
    if n % unit == 0 and n // unit > REDUCE_CHAINS:
        nv = n // unit
        chains = min((d for d in range(1, nv + 1) if nv % d == 0), key=lambda d: abs(d - REDUCE_CHAINS))
        width = unit * chains
        take = (lambda c: x[c:c + width, :]) if axis == 0 else (lambda c: x[:, c:c + width])
        acc = take(0)
        for c in range(width, n, width):
            acc = pair(acc, take(c))
        x = acc
    return full(x, axis=axis, keepdims=True)


def _topk_select(key, kpos, k_sel, axis):
    kf = float(k_sel)
    n_idx_bits = max(1, int(math.ceil(math.log2(key.shape[axis] + 1))))
    qshape = tuple(1 if a == axis else n for a, n in enumerate(key.shape))

    def count(mask):
        return _reduce(jnp.where(mask, 1.0, 0.0), "sum", axis)

    ans0 = jnp.where(count(key >= 0) >= kf, jnp.int32(0), jnp.int32(INT_MIN))

    def val_step(b, ans):
        cand = ans + jnp.left_shift(jnp.int32(1), jnp.int32(30) - b)
        return jnp.where(count(key >= cand) >= kf, cand, ans)

    thr = lax.fori_loop(0, 31, val_step, ans0)
    gt = key > thr
    eq = key == thr
    need = kf - count(gt)
    has_ties = jnp.max(count(eq) - need) > 0.0

    def tie_search():
        def idx_step(b, j):
            cand = j + jnp.left_shift(jnp.int32(1), jnp.int32(n_idx_bits - 1) - b)
            c = count(jnp.where(eq, kpos, jnp.int32(2 ** 30)) < cand)
            return jnp.where(c < need, cand, j)
        return lax.fori_loop(0, n_idx_bits, idx_step, jnp.zeros(qshape, I32))

    def no_ties():
        return jnp.full(qshape, 2 ** 30, I32)

    jthr = lax.cond(has_ties, tie_search, no_ties)
    tie_bias = jnp.where(kpos <= jthr, 0.0, NEG_INF)
    return jnp.where(gt, 0.0, jnp.where(eq, tie_bias, NEG_INF))


def _rope(x, cosf, sin_up, sin_dn):
    half = ROT_DIM // 2
    return (x * cosf + pltpu.roll(x, LANES - half, axis=1) * sin_up
            + pltpu.roll(x, half, axis=1) * sin_dn)


def _proj_kernel(x_ref, w_ref, cos_ref, sup_ref, sdn_ref,
                 q_ref, k_ref, v_ref, ga_ref, qi_ref, kw_ref, u_ref, gs_ref, *, aw, sw):
    xb = x_ref[...].astype(BF16)
    cosf, sup, sdn = cos_ref[...], sup_ref[...], sdn_ref[...]

    def seg(off, width):
        return jnp.dot(xb, w_ref[:, off:off + width], preferred_element_type=F32)

    def rope_seg(h):
        return jnp.concatenate(
            [_rope(h[:, c:c + LANES], cosf, sup, sdn) for c in range(0, h.shape[1], LANES)], axis=1)

    off = 0
    q_ref[...] = (rope_seg(seg(off, aw)) * (HEAD_DIM ** -0.5)).T.astype(BF16); off += aw
    k_ref[...] = rope_seg(seg(off, aw)); off += aw
    v_ref[...] = seg(off, aw); off += aw
    ga_ref[...] = seg(off, aw); off += aw
    qi_ref[...] = rope_seg(seg(off, N_IDX_HEADS * IDX_DIM)).T.astype(BF16); off += N_IDX_HEADS * IDX_DIM
    kw = seg(off, LANES); off += LANES
    lane = lax.broadcasted_iota(I32, kw.shape, 1)
    is_key = lane < IDX_DIM
    kw_ref[...] = _rope(kw, jnp.where(is_key, cosf, 1.0), jnp.where(is_key, sup, 0.0),
                        jnp.where(is_key, sdn, 0.0))
    u_ref[...] = seg(off, sw); off += sw
    gs_ref[...] = seg(off, sw)


def _rope_tables(pos):
    half = ROT_DIM // 2
    inv = ROPE_THETA ** (-2.0 * jnp.arange(half, dtype=F32) / ROT_DIM)
    ang = pos.astype(F32)[:, None] * inv[None, :]
    cos, sin = jnp.cos(ang), jnp.sin(ang)
    p = pos.shape[0]
    ones = jnp.ones((p, HEAD_DIM - ROT_DIM), F32)
    zeros = jnp.zeros((p, HEAD_DIM - ROT_DIM), F32)
    zh = jnp.zeros((p, half), F32)
    cos64 = jnp.concatenate([cos, cos, ones], axis=1)
    up64 = jnp.concatenate([-sin, zh, zeros], axis=1)
    dn64 = jnp.concatenate([zh, sin, zeros], axis=1)
    rep = LANES // HEAD_DIM
    return (jnp.tile(cos64, (1, rep)), jnp.tile(up64, (1, rep)), jnp.tile(dn64, (1, rep)))


def _project(x2, w_packed, tables, tm, aw, sw):
    rows, d = x2.shape
    n_tab_blocks = tables[0].shape[0] // tm
    row_spec = lambda width: pl.BlockSpec((tm, width), lambda i: (i, 0))
    col_spec = lambda height: pl.BlockSpec((height, tm), lambda i: (0, i))
    tab_spec = pl.BlockSpec((tm, LANES), lambda i: (i % n_tab_blocks, 0))
    idx_w = N_IDX_HEADS * IDX_DIM
    out_shapes = (
        jax.ShapeDtypeStruct((aw, rows), BF16),
        jax.ShapeDtypeStruct((rows, aw), F32),
        jax.ShapeDtypeStruct((rows, aw), F32),
        jax.ShapeDtypeStruct((rows, aw), F32),
        jax.ShapeDtypeStruct((idx_w, rows), BF16),
        jax.ShapeDtypeStruct((rows, LANES), F32),
        jax.ShapeDtypeStruct((rows, sw), F32),
        jax.ShapeDtypeStruct((rows, sw), F32),
    )
    out_specs = (col_spec(aw), row_spec(aw), row_spec(aw), row_spec(aw), col_spec(idx_w),
                 row_spec(LANES), row_spec(sw), row_spec(sw))
    return pl.pallas_call(
        functools.partial(_proj_kernel, aw=aw, sw=sw),
        grid=(rows // tm,),
        in_specs=[row_spec(d), pl.BlockSpec(w_packed.shape, lambda i: (0, 0)),
                  tab_spec, tab_spec, tab_spec],
        out_specs=out_specs,
        out_shape=out_shapes,
        compiler_params=_cparams("arbitrary"),
        name="in_proj",
    )(x2, w_packed, *tables)


KEY_CHUNK = 2048


def _prompt_attn_kernel(qt_ref, qit_ref, kwq_ref, k_ref, v_ref, kw_ref, o_ref,
                        kb_ref, vt_ref, kib_ref, key_ref, bias_ref, *, n_heads, k_sel, n_groups):
    i = pl.program_id(1)
    t = k_ref.shape[0]

    @pl.when(i == 0)
    def _():
        for h in range(n_heads):
            kb_ref[h] = k_ref[:, h * HEAD_DIM:(h + 1) * HEAD_DIM].astype(BF16)
        for c in range(0, t, LANES):
            vt = v_ref[c:c + LANES, :].T
            for h in range(n_heads):
                vt_ref[h, :, c:c + LANES] = vt[h * HEAD_DIM:(h + 1) * HEAD_DIM, :].astype(BF16)
        kib_ref[...] = kw_ref[:, :IDX_DIM].astype(BF16)

    w_scale = (IDX_DIM ** -0.5) * (N_IDX_HEADS ** -0.5)

    def body(s):
        kc = max(d for d in range(LANES, KEY_CHUNK + 1, LANES) if s % d == 0)
        w_t = kwq_ref[...].T[IDX_DIM:IDX_DIM + N_IDX_HEADS, :] * w_scale
        qpos = i * Q_BLOCK + lax.broadcasted_iota(I32, (kc, Q_BLOCK), 1)
        row = lax.broadcasted_iota(I32, (kc, Q_BLOCK), 0)
        for c in range(0, s, kc):
            kib = kib_ref[c:c + kc, :]
            score = None
            for h in range(N_IDX_HEADS):
                d = jnp.dot(kib, qit_ref[h * IDX_DIM:(h + 1) * IDX_DIM, :], preferred_element_type=F32)
                term = jnp.maximum(d, 0.0) * w_t[h:h + 1, :]
                score = term if score is None else score + term
            key_ref[c:c + kc, :] = jnp.where(c + row <= qpos, _ordered_key(score), jnp.int32(INT_MIN))

        kpos = lax.broadcasted_iota(I32, (s, Q_BLOCK), 0)
        valid = kpos <= i * Q_BLOCK + lax.broadcasted_iota(I32, (s, Q_BLOCK), 1)
        bias_ref[0:s, :] = jnp.where(valid, _topk_select(key_ref[0:s, :], kpos, k_sel, axis=0), NEG_INF)

        m, l, acc = [None] * n_heads, [None] * n_heads, [None] * n_heads
        for c in range(0, s, kc):
            bias = bias_ref[c:c + kc, :]
            for h in range(n_heads):
                sc = jnp.dot(kb_ref[h, c:c + kc, :], qt_ref[h * HEAD_DIM:(h + 1) * HEAD_DIM, :],
                             preferred_element_type=F32) + bias
                m_c = _reduce(sc, "max", 0)
                m_new = m_c if c == 0 else jnp.maximum(m[h], m_c)
                m_safe = jnp.where(m_new == NEG_INF, 0.0, m_new)
                p = jnp.exp(sc - m_safe)
                l_c = _reduce(p, "sum", 0)
                pv = jnp.dot(vt_ref[h, :, c:c + kc], p.astype(BF16), preferred_element_type=F32)
                if c == 0:
                    l[h], acc[h] = l_c, pv
                else:
                    alpha = jnp.exp(m[h] - m_safe)
                    l[h], acc[h] = alpha * l[h] + l_c, alpha * acc[h] + pv
                m[h] = m_new
        o_ref[...] = jnp.concatenate([acc[h] / l[h] for h in range(n_heads)], axis=0).T

    blocks_per_group = (t // Q_BLOCK) // n_groups
    for g in range(n_groups):
        pl.when(i // blocks_per_group == g)(
            functools.partial(body, (g + 1) * blocks_per_group * Q_BLOCK))


def _prompt_attention(q_t, qi_t, kw, k, v, batch, t, n_heads, k_sel):
    aw = n_heads * HEAD_DIM
    nq = t // Q_BLOCK
    n_groups = 4 if nq % 4 == 0 else 1
    qspec = lambda width: pl.BlockSpec((Q_BLOCK, width), lambda b, i: (b * nq + i, 0))
    tspec = lambda rows: pl.BlockSpec((rows, Q_BLOCK), lambda b, i: (0, b * nq + i))
    sspec = lambda width: pl.BlockSpec((t, width), lambda b, i: (b, 0))
    return pl.pallas_call(
        functools.partial(_prompt_attn_kernel, n_heads=n_heads, k_sel=k_sel, n_groups=n_groups),
        grid=(batch, nq),
        in_specs=[tspec(aw), tspec(N_IDX_HEADS * IDX_DIM), qspec(LANES),
                  sspec(aw), sspec(aw), sspec(LANES)],
        out_specs=qspec(aw),
        out_shape=jax.ShapeDtypeStruct((batch * t, aw), F32),
        scratch_shapes=[pltpu.VMEM((n_heads, t, HEAD_DIM), BF16),
                        pltpu.VMEM((n_heads, HEAD_DIM, t), BF16),
                        pltpu.VMEM((t, IDX_DIM), BF16),
                        pltpu.VMEM((t, Q_BLOCK), I32),
                        pltpu.VMEM((t, Q_BLOCK), F32)],
        compiler_params=_cparams("arbitrary", "arbitrary"),
        name="prompt_attn",
    )(q_t, qi_t, kw, k, v, kw)


SCAN_LANES = 512


def _ssm_prompt_kernel(u_ref, h0r_ref, h0i_ref, bwr_ref, bwi_ref, cwr_ref, cwi_ref, d_ref, tab_ref,
                       y_ref, hr_ref, hi_ref, br_scr, bi_scr, cr_scr, ci_scr):
    c = pl.program_id(1)
    tc = u_ref.shape[0]
    gn = br_scr.shape[1]

    @pl.when(c == 0)
    def _():
        cr_scr[...] = h0r_ref[0]
        ci_scr[...] = h0i_ref[0]

    u = u_ref[...]
    ub = u.astype(BF16)
    n_in = bwr_ref.shape[0]
    kw = bwr_ref.shape[1]
    nw = bwr_ref.shape[2]
    for j in range(n_in):
        uj = ub[:, j * kw:(j + 1) * kw]
        br_scr[:, j * nw:(j + 1) * nw] = jnp.dot(uj, bwr_ref[j], preferred_element_type=F32)
        bi_scr[:, j * nw:(j + 1) * nw] = jnp.dot(uj, bwi_ref[j], preferred_element_type=F32)

    for lo in range(0, gn, SCAN_LANES):
        sl = slice(lo, lo + SCAN_LANES)
        a1r, a1i = tab_ref[0, :, sl], tab_ref[1, :, sl]
        a2r, a2i = tab_ref[2, :, sl], tab_ref[3, :, sl]
        a4r, a4i = tab_ref[4, :, sl], tab_ref[5, :, sl]
        pr, pi = tab_ref[6, :, sl], tab_ref[7, :, sl]

        def tile(j, carry):
            cr, ci = carry
            rows = pl.ds(pl.multiple_of(j * SUBLANES, SUBLANES), SUBLANES)
            br, bi = br_scr[rows, sl], bi_scr[rows, sl]
            for shift, ar, ai in ((1, a1r, a1i), (2, a2r, a2i), (4, a4r, a4i)):
                sr = pltpu.roll(br, shift, axis=0)
                si = pltpu.roll(bi, shift, axis=0)
                br, bi = br + (ar * sr - ai * si), bi + (ar * si + ai * sr)
            hr = br + (pr * cr - pi * ci)
            hi = bi + (pr * ci + pi * cr)
            br_scr[rows, sl] = hr
            bi_scr[rows, sl] = hi
            return hr[SUBLANES - 1:SUBLANES, :], hi[SUBLANES - 1:SUBLANES, :]

        cr, ci = lax.fori_loop(0, tc // SUBLANES, tile, (cr_scr[:, sl], ci_scr[:, sl]))
        cr_scr[:, sl] = cr
        ci_scr[:, sl] = ci

    y = (jnp.dot(br_scr[...].astype(BF16), cwr_ref[...], preferred_element_type=F32)
         - jnp.dot(bi_scr[...].astype(BF16), cwi_ref[...], preferred_element_type=F32))
    y_ref[...] = y + d_ref[...] * u
    hr_ref[0] = cr_scr[...]
    hi_ref[0] = ci_scr[...]


def _ssm_prompt(u, h0r, h0i, ssm_w, batch, t, tc):
    bwr, bwi, cwr, cwi, d_skip, tab = ssm_w
    sw = u.shape[1]
    gn = cwr.shape[0]
    nchunk = t // tc
    full = lambda a: pl.BlockSpec(a.shape, lambda b, c: (0,) * a.ndim)
    st_spec = pl.BlockSpec((1, 1, gn), lambda b, c: (b, 0, 0))
    u_spec = pl.BlockSpec((tc, sw), lambda b, c: (b * nchunk + c, 0))
    return pl.pallas_call(
        _ssm_prompt_kernel,
        grid=(batch, nchunk),
        in_specs=[u_spec, st_spec, st_spec, full(bwr), full(bwi), full(cwr), full(cwi),
                  full(d_skip), full(tab)],
        out_specs=(u_spec, st_spec, st_spec),
        out_shape=(jax.ShapeDtypeStruct((batch * t, sw), F32),
                   jax.ShapeDtypeStruct((batch, 1, gn), F32),
                   jax.ShapeDtypeStruct((batch, 1, gn), F32)),
        scratch_shapes=[pltpu.VMEM((tc, gn), F32), pltpu.VMEM((tc, gn), F32),
                        pltpu.VMEM((1, gn), F32), pltpu.VMEM((1, gn), F32)],
        compiler_params=_cparams("arbitrary", "arbitrary"),
        name="ssm_prompt",
    )(u, h0r, h0i, bwr, bwi, cwr, cwi, d_skip, tab)


def _ssm_sample_kernel(u_ref, h0r_ref, h0i_ref, bwr_ref, bwi_ref, cwr_ref, cwi_ref, d_ref, tab_ref,
                       y_ref, hr_ref, hi_ref, *, steps):
    sw = d_ref.shape[1]
    n_in, kw, nw = bwr_ref.shape
    ar, ai = tab_ref[6, 0:1, :], tab_ref[7, 0:1, :]
    hr, hi = h0r_ref[...], h0i_ref[...]
    for s in range(steps):
        u = u_ref[:, s * sw:(s + 1) * sw]
        ub = u.astype(BF16)
        br = jnp.concatenate([jnp.dot(ub[:, j * kw:(j + 1) * kw], bwr_ref[j], preferred_element_type=F32)
                              for j in range(n_in)], axis=1)
        bi = jnp.concatenate([jnp.dot(ub[:, j * kw:(j + 1) * kw], bwi_ref[j], preferred_element_type=F32)
                              for j in range(n_in)], axis=1)
        hr, hi = (ar * hr - ai * hi) + br, (ar * hi + ai * hr) + bi
        y = (jnp.dot(hr.astype(BF16), cwr_ref[...], preferred_element_type=F32)
             - jnp.dot(hi.astype(BF16), cwi_ref[...], preferred_element_type=F32))
        y_ref[:, s * sw:(s + 1) * sw] = y + d_ref[...] * u
    hr_ref[...] = hr
    hi_ref[...] = hi


def _ssm_sample(u2, h0r, h0i, ssm_w, steps):
    bwr, bwi, cwr, cwi, d_skip, tab = ssm_w
    nseq = u2.shape[0]
    gn = cwr.shape[0]
    return pl.pallas_call(
        functools.partial(_ssm_sample_kernel, steps=steps),
        out_shape=(jax.ShapeDtypeStruct(u2.shape, F32),
                   jax.ShapeDtypeStruct((nseq, gn), F32),
                   jax.ShapeDtypeStruct((nseq, gn), F32)),
        compiler_params=pltpu.CompilerParams(vmem_limit_bytes=VMEM_LIMIT),
        name="ssm_sample",
    )(u2, h0r, h0i, bwr, bwi, cwr, cwi, d_skip, tab)


def _finish_kernel(x_ref, at_ref, ga_ref, ys_ref, gs_ref, wglu_ref, bglu_ref, woa_ref, wos_ref,
                   lng_ref, lnb_ref, o_ref, *, alpha):
    a = at_ref[...] * jax.nn.silu(ga_ref[...])
    g = jax.nn.gelu(ys_ref[...], approximate=True)
    z = jnp.dot(g.astype(BF16), wglu_ref[...], preferred_element_type=F32) + bglu_ref[...]
    s = g * jax.nn.sigmoid(z) * jax.nn.silu(gs_ref[...])
    mix = (jnp.dot(a.astype(BF16), woa_ref[...], preferred_element_type=F32)
           + jnp.dot(s.astype(BF16), wos_ref[...], preferred_element_type=F32))
    r = alpha * x_ref[...] + mix
    mean = jnp.mean(r, axis=-1, keepdims=True)
    cen = r - mean
    var = jnp.mean(cen * cen, axis=-1, keepdims=True)
    o_ref[...] = cen * lax.rsqrt(var + LN_EPS) * lng_ref[...] + lnb_ref[...]


def _finish(x2, attn, g_att, y_ssm, g_ssm, fin_w, tm, alpha):
    rows, d = x2.shape
    row_spec = lambda a: pl.BlockSpec((tm, a.shape[1]), lambda i: (i, 0))
    full = lambda a: pl.BlockSpec(a.shape, lambda i: (0,) * a.ndim)
    acts = (x2, attn, g_att, y_ssm, g_ssm)
    return pl.pallas_call(
        functools.partial(_finish_kernel, alpha=alpha),
        grid=(rows // tm,),
        in_specs=[row_spec(a) for a in acts] + [full(w) for w in fin_w],
        out_specs=pl.BlockSpec((tm, d), lambda i: (i, 0)),
        out_shape=jax.ShapeDtypeStruct((rows, d), F32),
        compiler_params=_cparams("arbitrary"),
        name="finish",
    )(*acts, *fin_w)


def _page_copy(pt_ref, src_hbm, buf, sem, seq, first_page, slot, buf_off, j):
    return pltpu.make_async_copy(src_hbm.at[pt_ref[seq, first_page + j]], buf.at[slot, buf_off + j],
                                 sem.at[slot])


def _start_pages(n, *copy_args):
    def go(j, carry):
        _page_copy(*copy_args, j).start()
        return carry
    lax.fori_loop(0, n, go, 0)


def _wait_pages(n, *copy_args):
    def go(j, carry):
        _page_copy(*copy_args, j).wait()
        return carry
    lax.fori_loop(0, n, go, 0)


def _sample_index_kernel(pt_ref, qi_ref, wi_ref, kin_ref, kidx_hbm, bias_ref, buf, sem,
                         *, n_pages, page, steps, k_sel, group):
    s = pl.program_id(0)
    nsteps = pl.num_programs(0)
    slot = s % 2

    def for_group(fn, step, sl):
        for g in range(group):
            fn(n_pages, pt_ref, kidx_hbm, buf, sem, step * group + g, 0, sl, g * n_pages)

    @pl.when(s == 0)
    def _():
        for_group(_start_pages, 0, 0)

    @pl.when(s + 1 < nsteps)
    def _():
        for_group(_start_pages, s + 1, 1 - slot)

    for_group(_wait_pages, s, slot)

    past = n_pages * page
    nt = (((1,), (1,)), ((), ()))
    w_scale = (IDX_DIM ** -0.5) * (N_IDX_HEADS ** -0.5)
    pad = jnp.zeros((LANES - SUBLANES, IDX_DIM), F32)

    def head_sum(d, wi):
        w = jnp.maximum(d, 0.0) * wi
        out = w[0:SUBLANES]
        for h in range(1, N_IDX_HEADS):
            out = out + w[h * SUBLANES:(h + 1) * SUBLANES]
        return out

    scores = []
    for g in range(group):
        qi = qi_ref[g]
        wi = wi_ref[g] * w_scale
        ki_t = jnp.concatenate([buf[slot, g * n_pages + j].astype(BF16) for j in range(n_pages)],
                               axis=1)
        ki_new = jnp.concatenate([kin_ref[g][:, :IDX_DIM], pad], axis=0).astype(BF16)
        sc_past = head_sum(jnp.dot(qi, ki_t, preferred_element_type=F32), wi)
        sc_new = head_sum(lax.dot_general(qi, ki_new, nt, preferred_element_type=F32), wi)
        scores.append(jnp.concatenate([sc_past, sc_new], axis=1))
    score = jnp.concatenate(scores, axis=0)
    shape = score.shape
    kpos = lax.broadcasted_iota(I32, shape, 1)
    qstep = lax.broadcasted_iota(I32, shape, 0) & (SUBLANES - 1)
    valid = kpos <= jnp.minimum(past + qstep, past + steps - 1)
    key = jnp.where(valid, _ordered_key(score), jnp.int32(INT_MIN))
    key = jnp.where(qstep < steps, key, -kpos)
    bias = jnp.where(valid, _topk_select(key, kpos, k_sel, axis=1), NEG_INF)
    for g in range(group):
        bias_ref[g] = bias[g * SUBLANES:(g + 1) * SUBLANES]


def _sample_index(page_table, qi_hs, wi_hs, kw_new, kidx_pages, steps, k_sel, group):
    nseq, n_pages = page_table.shape
    page = kidx_pages.shape[2]
    past = n_pages * page
    blk = lambda a: pl.BlockSpec((group,) + a.shape[1:], lambda s, pt: (s, 0, 0))
    grid_spec = pltpu.PrefetchScalarGridSpec(
        num_scalar_prefetch=1,
        grid=(nseq // group,),
        in_specs=[blk(qi_hs), blk(wi_hs), blk(kw_new), pl.BlockSpec(memory_space=pl.ANY)],
        out_specs=pl.BlockSpec((group, SUBLANES, past + LANES), lambda s, pt: (s, 0, 0)),
        scratch_shapes=[pltpu.VMEM((2, group * n_pages, IDX_DIM, page), F32),
                        pltpu.SemaphoreType.DMA((2,))],
    )
    return pl.pallas_call(
        functools.partial(_sample_index_kernel, n_pages=n_pages, page=page, steps=steps, k_sel=k_sel,
                          group=group),
        grid_spec=grid_spec,
        out_shape=jax.ShapeDtypeStruct((nseq, SUBLANES, past + LANES), F32),
        compiler_params=_cparams("arbitrary"),
        name="sample_index",
    )(page_table, qi_hs, wi_hs, kw_new, kidx_pages)


def _sample_attn_kernel(pt_ref, q_ref, bias_ref, biasn_ref, kn_ref, vn_ref, k_hbm, v_hbm, o_ref,
                        kbuf, vbuf, ksem, vsem, m_scr, l_scr, acc_scr,
                        *, n_chunks, cpages, page, n_heads):
    step = pl.program_id(0)
    nsteps = pl.num_programs(0)
    slot = step % 2

    def both(fn, item, sl):
        seq, ch = item // n_chunks, item % n_chunks
        fn(cpages, pt_ref, k_hbm, kbuf, ksem, seq, ch * cpages, sl, 0)
        fn(cpages, pt_ref, v_hbm, vbuf, vsem, seq, ch * cpages, sl, 0)

    @pl.when(step == 0)
    def _():
        both(_start_pages, step, 0)

    @pl.when(step + 1 < nsteps)
    def _():
        both(_start_pages, step + 1, 1 - slot)

    both(_wait_pages, step, slot)
    ch = step % n_chunks

    @pl.when(ch == 0)
    def _():
        m_scr[...] = jnp.full(m_scr.shape, NEG_INF, F32)
        l_scr[...] = jnp.zeros(l_scr.shape, F32)
        acc_scr[...] = jnp.zeros(acc_scr.shape, F32)

    nt = (((1,), (1,)), ((), ()))

    def update(k_t, v_t, bias8):
        sc = jnp.concatenate([jnp.dot(q_ref[0, h], k_t[h], preferred_element_type=F32)
                              for h in range(n_heads)], axis=0)
        sc = sc + jnp.concatenate([bias8] * n_heads, axis=0)
        m_old = m_scr[...]
        m_new = jnp.maximum(m_old, jnp.max(sc, axis=1, keepdims=True))
        m_safe = jnp.where(m_new == NEG_INF, 0.0, m_new)
        alpha = jnp.exp(m_old - m_safe)
        p = jnp.exp(sc - m_safe)
        l_scr[...] = alpha * l_scr[...] + jnp.sum(p, axis=1, keepdims=True)
        pv = jnp.concatenate([lax.dot_general(p[h * SUBLANES:(h + 1) * SUBLANES].astype(BF16), v_t[h], nt,
                                              preferred_element_type=F32)
                              for h in range(n_heads)], axis=0)
        acc_scr[...] = alpha * acc_scr[...] + pv
        m_scr[...] = m_new

    def chunk_t(buf, h):
        return jnp.concatenate([buf[slot, j, h].astype(BF16) for j in range(cpages)], axis=1)

    update([chunk_t(kbuf, h) for h in range(n_heads)], [chunk_t(vbuf, h) for h in range(n_heads)],
           bias_ref[0])

    @pl.when(ch == n_chunks - 1)
    def _():
        pad = jnp.zeros((LANES - SUBLANES, n_heads * HEAD_DIM), F32)

        def new_t(ref):
            x_t = jnp.concatenate([ref[0], pad], axis=0).T
            return [x_t[h * HEAD_DIM:(h + 1) * HEAD_DIM].astype(BF16) for h in range(n_heads)]

        update(new_t(kn_ref), new_t(vn_ref), biasn_ref[0])
        o = acc_scr[...] / l_scr[...]
        o_ref[0] = jnp.concatenate([o[h * SUBLANES:(h + 1) * SUBLANES] for h in range(n_heads)], axis=1)


def _sample_attention(page_table, q_hs, bias, k_new, v_new, k_pages, v_pages, cpages):
    nseq, n_pages = page_table.shape
    n_heads, page = k_pages.shape[1], k_pages.shape[3]
    aw = n_heads * HEAD_DIM
    n_chunks = n_pages // cpages
    cs = cpages * page
    rows = n_heads * SUBLANES
    grid_spec = pltpu.PrefetchScalarGridSpec(
        num_scalar_prefetch=1,
        grid=(nseq * n_chunks,),
        in_specs=[pl.BlockSpec((1, n_heads, SUBLANES, HEAD_DIM), lambda i, pt: (i // n_chunks, 0, 0, 0)),
                  pl.BlockSpec((1, SUBLANES, cs), lambda i, pt: (i // n_chunks, 0, i % n_chunks)),
                  pl.BlockSpec((1, SUBLANES, LANES), lambda i, pt: (i // n_chunks, 0, n_pages * page // LANES)),
                  pl.BlockSpec((1, SUBLANES, aw), lambda i, pt: (i // n_chunks, 0, 0)),
                  pl.BlockSpec((1, SUBLANES, aw), lambda i, pt: (i // n_chunks, 0, 0)),
                  pl.BlockSpec(memory_space=pl.ANY),
                  pl.BlockSpec(memory_space=pl.ANY)],
        out_specs=pl.BlockSpec((1, SUBLANES, aw), lambda i, pt: (i // n_chunks, 0, 0)),
        scratch_shapes=[pltpu.VMEM((2, cpages, n_heads, HEAD_DIM, page), F32),
                        pltpu.VMEM((2, cpages, n_heads, HEAD_DIM, page), F32),
                        pltpu.SemaphoreType.DMA((2,)),
                        pltpu.SemaphoreType.DMA((2,)),
                        pltpu.VMEM((rows, 1), F32),
                        pltpu.VMEM((rows, 1), F32),
                        pltpu.VMEM((rows, HEAD_DIM), F32)],
    )
    return pl.pallas_call(
        functools.partial(_sample_attn_kernel, n_chunks=n_chunks, cpages=cpages, page=page,
                          n_heads=n_heads),
        grid_spec=grid_spec,
        out_shape=jax.ShapeDtypeStruct((nseq, SUBLANES, aw), F32),
        compiler_params=_cparams("arbitrary"),
        name="sample_attn",
    )(page_table, q_hs, bias, bias, k_new, v_new, k_pages, v_pages)


def _pack_w_in(w, aw, sw):
    sizes = [aw] * 4 + [N_IDX_HEADS * IDX_DIM, IDX_DIM, N_IDX_HEADS, sw, sw]
    offs = np.cumsum([0] + sizes)
    seg = [w[:, offs[j]:offs[j + 1]] for j in range(len(sizes))]
    pad = jnp.zeros((w.shape[0], LANES - IDX_DIM - N_IDX_HEADS), w.dtype)
    return jnp.concatenate(seg[:5] + [seg[5], seg[6], pad] + seg[7:], axis=1).astype(BF16)


def _ssm_weights(a_re, a_im, log_dt, b_re, b_im, c_re, c_im, d_skip):
    g, n, c = b_re.shape
    a_re, a_im = a_re.astype(F32), a_im.astype(F32)
    dt = jnp.exp(log_dt.astype(F32))[:, None]
    mag = jnp.exp(dt * a_re)
    abr = mag * jnp.cos(dt * a_im)
    abi = mag * jnp.sin(dt * a_im)
    den = a_re * a_re + a_im * a_im
    nr = abr - 1.0
    cr = (nr * a_re + abi * a_im) / den
    ci = (abi * a_re - nr * a_im) / den
    b_re, b_im = b_re.astype(F32), b_im.astype(F32)
    bbr = cr[..., None] * b_re - ci[..., None] * b_im
    bbi = cr[..., None] * b_im + ci[..., None] * b_re

    kw_lanes = 2 * LANES
    gpk = kw_lanes // c
    n_in = g // gpk
    eye = jnp.eye(gpk, dtype=F32)

    def in_map(bb):
        bb = bb.reshape(n_in, gpk, n, c)
        return jnp.einsum("jgnc,gh->jgchn", bb, eye).reshape(n_in, gpk * c, gpk * n).astype(BF16)

    def out_map(cc):
        return jnp.einsum("gcn,gh->gnhc", cc.astype(F32), jnp.eye(g, dtype=F32)).reshape(g * n, g * c).astype(BF16)

    ar, ai = abr.reshape(1, g * n), abi.reshape(1, g * n)
    pows_r, pows_i = [ar], [ai]
    for _ in range(SUBLANES - 1):
        pr, pi = pows_r[-1], pows_i[-1]
        pows_r.append(pr * ar - pi * ai)
        pows_i.append(pr * ai + pi * ar)
    row = jnp.arange(SUBLANES)[:, None]

    def masked(k):
        keep = row >= k
        return (jnp.where(keep, pows_r[k - 1], 0.0), jnp.where(keep, pows_i[k - 1], 0.0))

    m1, m2, m4 = masked(1), masked(2), masked(4)
    tab = jnp.stack([m1[0], m1[1], m2[0], m2[1], m4[0], m4[1],
                     jnp.concatenate(pows_r, axis=0), jnp.concatenate(pows_i, axis=0)], axis=0)
    return (in_map(bbr), in_map(bbi), out_map(c_re), out_map(c_im),
            d_skip.astype(F32).reshape(1, g * c), tab)


def kernel(x_prompt, x_sample, cache_k, cache_v, cache_kidx, page_table, state_ssm_re, state_ssm_im,
           w_in, w_out, ssm_a_re, ssm_a_im, ssm_log_dt, ssm_b_re, ssm_b_im, ssm_c_re, ssm_c_im,
           ssm_d, w_glu, b_glu, ln_g, ln_b):
    depth = w_in.shape[0]
    assert depth == 1, "single-layer trunk"
    batch, t, d = x_prompt.shape
    nseq, steps, _ = x_sample.shape
    n_pool, page, n_heads, head_dim = cache_k.shape[1:]
    assert head_dim == HEAD_DIM and steps <= SUBLANES
    aw = n_heads * HEAD_DIM
    g_ssm_n = ssm_b_re.shape[1]
    sw = g_ssm_n * SSM_GROUP
    gn = g_ssm_n * SSM_STATE
    n_pages = page_table.shape[1]
    past = n_pages * page
    alpha = (2.0 * depth) ** 0.25

    layer = 0
    w_packed = _pack_w_in(w_in[layer], aw, sw)
    ssm_w = _ssm_weights(ssm_a_re[layer], ssm_a_im[layer], ssm_log_dt[layer], ssm_b_re[layer],
                         ssm_b_im[layer], ssm_c_re[layer], ssm_c_im[layer], ssm_d[layer])
    fin_w = (w_glu[layer].astype(BF16), b_glu[layer].astype(F32).reshape(1, sw),
             w_out[layer][:aw].astype(BF16), w_out[layer][aw:].astype(BF16),
             ln_g[layer].astype(F32).reshape(1, d), ln_b[layer].astype(F32).reshape(1, d))

    tm = 512
    xp2 = x_prompt.reshape(batch * t, d)
    q, k, v, g_att, qi, kw, u, g_ssm = _project(xp2, w_packed, _rope_tables(jnp.arange(t)), tm, aw, sw)
    attn = _prompt_attention(q, qi, kw, k, v, batch, t, n_heads, min(TOPK_MAX, t // 4))
    zeros_state = jnp.zeros((batch, 1, gn), F32)
    y_ssm, hr_p, hi_p = _ssm_prompt(u, zeros_state, zeros_state, ssm_w, batch, t, min(512, t))
    y_prompt = _finish(xp2, attn, g_att, y_ssm, g_ssm, fin_w, tm, alpha).reshape(batch, t, d)
    new_k_p = k.reshape(1, batch, t, n_heads, HEAD_DIM)
    new_v_p = v.reshape(1, batch, t, n_heads, HEAD_DIM)
    new_ki_p = kw[:, :IDX_DIM].reshape(1, batch, t, IDX_DIM)
    new_hr_p = hr_p.reshape(1, batch, g_ssm_n, SSM_STATE)
    new_hi_p = hi_p.reshape(1, batch, g_ssm_n, SSM_STATE)

    rows_s = nseq * steps
    xs2 = x_sample.reshape(rows_s, d)
    pos_s = past + (jnp.arange(rows_s) % steps)
    qs, ks, vs, gas, qis, kws, us, gss = _project(xs2, w_packed, _rope_tables(pos_s), rows_s, aw, sw)

    def head_rows(a, width):
        a = a.reshape(nseq, steps, -1, width).transpose(0, 2, 1, 3)
        return jnp.pad(a, ((0, 0), (0, 0), (0, SUBLANES - steps), (0, 0)))

    def step_rows(a):
        return jnp.pad(a.reshape(nseq, steps, -1), ((0, 0), (0, SUBLANES - steps), (0, 0)))

    qi_hs = head_rows(qis.T, IDX_DIM).reshape(nseq, N_IDX_HEADS * SUBLANES, IDX_DIM)
    wi_hs = head_rows(kws[:, IDX_DIM:IDX_DIM + N_IDX_HEADS], 1).reshape(nseq, N_IDX_HEADS * SUBLANES, 1)
    q_hs = head_rows(qs.T, HEAD_DIM)

    kidx_t = jnp.transpose(cache_kidx[layer], (0, 2, 1))
    k_t = jnp.transpose(cache_k[layer], (0, 2, 3, 1))
    v_t = jnp.transpose(cache_v[layer], (0, 2, 3, 1))

    k_sel_s = min(TOPK_MAX, (past + steps) // 4)
    group = 4 if nseq % 4 == 0 else 1
    bias = _sample_index(page_table, qi_hs, wi_hs, step_rows(kws), kidx_t, steps, k_sel_s, group)
    cpages = 16 if n_pages % 16 == 0 else n_pages
    attn_s = _sample_attention(page_table, q_hs, bias, step_rows(ks), step_rows(vs), k_t, v_t, cpages)
    attn_s = attn_s[:, :steps].reshape(rows_s, aw)

    y_s, hr_s, hi_s = _ssm_sample(us.reshape(nseq, steps * sw),
                                  state_ssm_re[layer].reshape(nseq, gn).astype(F32),
                                  state_ssm_im[layer].reshape(nseq, gn).astype(F32), ssm_w, steps)
    y_sample = _finish(xs2, attn_s, gas, y_s.reshape(rows_s, sw), gss, fin_w, rows_s, alpha)
    y_sample = y_sample.reshape(nseq, steps, d)

    return (y_prompt, y_sample, new_k_p, new_v_p, new_ki_p, new_hr_p, new_hi_p,
            ks.reshape(1, nseq, steps, n_heads, HEAD_DIM), vs.reshape(1, nseq, steps, n_heads, HEAD_DIM),
            kws[:, :IDX_DIM].reshape(1, nseq, steps, IDX_DIM),
            hr_s.reshape(1, nseq, g_ssm_n, SSM_STATE), hi_s.reshape(1, nseq, g_ssm_n, SSM_STATE))
```

```python
import functools
import math

import jax
import jax.numpy as jnp
import numpy as np
from jax import lax
from jax.experimental import pallas as pl
from jax.experimental.pallas import tpu as pltpu

F32 = jnp.float32
BF16 = jnp.bfloat16
I32 = jnp.int32

HEAD_DIM = 64
ROT_DIM = HEAD_DIM // 4
ROPE_THETA = 500000.0
N_IDX_HEADS = 8
IDX_DIM = 64
TOPK_MAX = 256
Q_BLOCK = 128
SSM_GROUP = 16
SSM_STATE = 64
LN_EPS = 1e-5

LANES = 128
SUBLANES = 8
VMEM_LIMIT = 52 * 1024 * 1024

INT_MIN = -(2 ** 31)
NEG_INF = float("-inf")


def _cparams(*sem):
    return pltpu.CompilerParams(dimension_semantics=sem, vmem_limit_bytes=VMEM_LIMIT)


def _ordered_to_float(o):
    return pltpu.bitcast(jnp.where(o < 0, o ^ jnp.int32(0x7FFFFFFF), o), F32)


REDUCE_CHAINS = 8


def _reduce(x, kind, axis):
    pair, full = (jnp.add, jnp.sum) if kind == "sum" else (jnp.maximum, jnp.max)
    n = x.shape[axis]
    unit = SUBLANES if axis == 0 else LANES
    if n % unit == 0 and n // unit > REDUCE_CHAINS:
        nv = n // unit
        chains = min((d for d in range(1, nv + 1) if nv % d == 0), key=lambda d: abs(d - REDUCE_CHAINS))
        width = unit * chains
        take = (lambda c: x[c:c + width, :]) if axis == 0 else (lambda c: x[:, c:c + width])
        acc = take(0)
        for c in range(width, n, width):
            acc = pair(acc, take(c))
        x = acc
    return full(x, axis=axis, keepdims=True)


def _topk_select(score, kpos, k_sel, axis):
    kf = float(k_sel)
    n_idx_bits = max(1, int(math.ceil(math.log2(score.shape[axis] + 1))))
    qshape = tuple(1 if a == axis else n for a, n in enumerate(score.shape))

    def count(mask):
        return _reduce(jnp.where(mask, 1.0, 0.0), "sum", axis)

    ans0 = jnp.where(count(score >= 0.0) >= kf, jnp.int32(0), jnp.int32(INT_MIN))

    def val_step(b, ans):
        cand = ans + jnp.left_shift(jnp.int32(1), jnp.int32(30) - b)
        return jnp.where(count(score >= _ordered_to_float(cand)) >= kf, cand, ans)

    enough = count(score > NEG_INF) >= kf
    thr = jnp.where(enough, _ordered_to_float(lax.fori_loop(0, 31, val_step, ans0)), NEG_INF)
    gt = score > thr
    eq = score == thr
    need = kf - count(gt)
    has_ties = jnp.max(jnp.where(enough, count(eq) - need, 0.0)) > 0.0

    def tie_search():
        def idx_step(b, j):
            cand = j + jnp.left_shift(jnp.int32(1), jnp.int32(n_idx_bits - 1) - b)
            c = count(jnp.where(eq, kpos, jnp.int32(2 ** 30)) < cand)
            return jnp.where(c < need, cand, j)
        return lax.fori_loop(0, n_idx_bits, idx_step, jnp.zeros(qshape, I32))

    def no_ties():
        return jnp.full(qshape, 2 ** 30, I32)

    jthr = lax.cond(has_ties, tie_search, no_ties)
    tie_bias = jnp.where(kpos <= jthr, 0.0, NEG_INF)
    return jnp.where(gt, 0.0, jnp.where(eq, tie_bias, NEG_INF))


def _rope(x, cosf, sin_up, sin_dn):
    half = ROT_DIM // 2
    return (x * cosf + pltpu.roll(x, LANES - half, axis=1) * sin_up
            + pltpu.roll(x, half, axis=1) * sin_dn)


def _proj_kernel(x_ref, w_ref, cos_ref, sup_ref, sdn_ref,
                 q_ref, k_ref, v_ref, ga_ref, qi_ref, kw_ref, ki_ref, u_ref, gs_ref, v_scr, *, aw, sw):
    xb = x_ref[...].astype(BF16)
    cosf, sup, sdn = cos_ref[...], sup_ref[...], sdn_ref[...]

    def seg(off, width):
        return jnp.dot(xb, w_ref[:, off:off + width], preferred_element_type=F32)

    def rope_seg(h):
        return jnp.concatenate(
            [_rope(h[:, c:c + LANES], cosf, sup, sdn) for c in range(0, h.shape[1], LANES)], axis=1)

    off = 0
    q_ref[...] = (rope_seg(seg(off, aw)) * (HEAD_DIM ** -0.5)).T.astype(BF16); off += aw
    k_ref[0] = rope_seg(seg(off, aw)).T; off += aw
    v_scr[...] = seg(off, aw); off += aw
    v_ref[0] = v_scr[...].T
    ga_ref[...] = seg(off, aw); off += aw
    qi_ref[...] = rope_seg(seg(off, N_IDX_HEADS * IDX_DIM)).T.astype(BF16); off += N_IDX_HEADS * IDX_DIM
    kw = seg(off, LANES); off += LANES
    lane = lax.broadcasted_iota(I32, kw.shape, 1)
    is_key = lane < IDX_DIM
    kw = _rope(kw, jnp.where(is_key, cosf, 1.0), jnp.where(is_key, sup, 0.0),
               jnp.where(is_key, sdn, 0.0))
    kw_ref[...] = kw
    ki_ref[0] = kw.T[:IDX_DIM, :]
    u_ref[...] = seg(off, sw); off += sw
    gs_ref[...] = seg(off, sw)


def _rope_tables(pos):
    half = ROT_DIM // 2
    inv = ROPE_THETA ** (-2.0 * jnp.arange(half, dtype=F32) / ROT_DIM)
    ang = pos.astype(F32)[:, None] * inv[None, :]
    cos, sin = jnp.cos(ang), jnp.sin(ang)
    p = pos.shape[0]
    ones = jnp.ones((p, HEAD_DIM - ROT_DIM), F32)
    zeros = jnp.zeros((p, HEAD_DIM - ROT_DIM), F32)
    zh = jnp.zeros((p, half), F32)
    cos64 = jnp.concatenate([cos, cos, ones], axis=1)
    up64 = jnp.concatenate([-sin, zh, zeros], axis=1)
    dn64 = jnp.concatenate([zh, sin, zeros], axis=1)
    rep = LANES // HEAD_DIM
    return (jnp.tile(cos64, (1, rep)), jnp.tile(up64, (1, rep)), jnp.tile(dn64, (1, rep)))


def _project(x2, w_packed, tables, tm, nseq, aw, sw):
    rows, d = x2.shape
    t = rows // nseq
    nblk = t // tm
    n_tab_blocks = tables[0].shape[0] // tm
    row_spec = lambda width: pl.BlockSpec((tm, width), lambda i: (i, 0))
    col_spec = lambda height: pl.BlockSpec((height, tm), lambda i: (0, i))
    seq_spec = lambda height: pl.BlockSpec((1, height, tm), lambda i: (i // nblk, 0, i % nblk))
    tab_spec = pl.BlockSpec((tm, LANES), lambda i: (i % n_tab_blocks, 0))
    idx_w = N_IDX_HEADS * IDX_DIM
    out_shapes = (
        jax.ShapeDtypeStruct((aw, rows), BF16),
        jax.ShapeDtypeStruct((nseq, aw, t), F32),
        jax.ShapeDtypeStruct((nseq, aw, t), F32),
        jax.ShapeDtypeStruct((rows, aw), F32),
        jax.ShapeDtypeStruct((idx_w, rows), BF16),
        jax.ShapeDtypeStruct((rows, LANES), F32),
        jax.ShapeDtypeStruct((nseq, IDX_DIM, t), F32),
        jax.ShapeDtypeStruct((rows, sw), F32),
        jax.ShapeDtypeStruct((rows, sw), F32),
    )
    out_specs = (col_spec(aw), seq_spec(aw), seq_spec(aw), row_spec(aw), col_spec(idx_w),
                 row_spec(LANES), seq_spec(IDX_DIM), row_spec(sw), row_spec(sw))
    return pl.pallas_call(
        functools.partial(_proj_kernel, aw=aw, sw=sw),
        grid=(rows // tm,),
        in_specs=[row_spec(d), pl.BlockSpec(w_packed.shape, lambda i: (0, 0)),
                  tab_spec, tab_spec, tab_spec],
        out_specs=out_specs,
        out_shape=out_shapes,
        scratch_shapes=[pltpu.VMEM((tm, aw), F32)],
        compiler_params=_cparams("arbitrary"),
        name="in_proj",
    )(x2, w_packed, *tables)


KEY_CHUNK = 2048


def _prompt_attn_kernel(qt_ref, qit_ref, kwq_ref, k_ref, v_ref, kw_ref, o_ref,
                        kb_ref, vt_ref, kib_ref, score_ref, bias_ref, *, n_heads, k_sel, n_groups):
    i = pl.program_id(1)
    t = kw_ref.shape[0]

    @pl.when(i == 0)
    def _():
        for h in range(n_heads):
            vt_ref[h] = v_ref[0, h * HEAD_DIM:(h + 1) * HEAD_DIM, :].astype(BF16)
        for c in range(0, t, LANES):
            kc_rows = k_ref[0, :, c:c + LANES].T
            for h in range(n_heads):
                kb_ref[h, c:c + LANES, :] = kc_rows[:, h * HEAD_DIM:(h + 1) * HEAD_DIM].astype(BF16)
        kib_ref[...] = kw_ref[:, :IDX_DIM].astype(BF16)

    w_scale = (IDX_DIM ** -0.5) * (N_IDX_HEADS ** -0.5)

    def body(s):
        kc = max(d for d in range(LANES, KEY_CHUNK + 1, LANES) if s % d == 0)
        w_t = kwq_ref[...].T[IDX_DIM:IDX_DIM + N_IDX_HEADS, :] * w_scale
        qpos = i * Q_BLOCK + lax.broadcasted_iota(I32, (kc, Q_BLOCK), 1)
        row = lax.broadcasted_iota(I32, (kc, Q_BLOCK), 0)
        for c in range(0, s, kc):
            kib = kib_ref[c:c + kc, :]
            score = None
            for h in range(N_IDX_HEADS):
                d = jnp.dot(kib, qit_ref[h * IDX_DIM:(h + 1) * IDX_DIM, :], preferred_element_type=F32)
                term = jnp.maximum(d, 0.0) * w_t[h:h + 1, :]
                score = term if score is None else score + term
            score_ref[c:c + kc, :] = jnp.where(c + row <= qpos, score, NEG_INF)

        kpos = lax.broadcasted_iota(I32, (s, Q_BLOCK), 0)
        valid = kpos <= i * Q_BLOCK + lax.broadcasted_iota(I32, (s, Q_BLOCK), 1)
        bias_ref[0:s, :] = jnp.where(valid, _topk_select(score_ref[0:s, :], kpos, k_sel, axis=0), NEG_INF)

        m, l, acc = [None] * n_heads, [None] * n_heads, [None] * n_heads
        for c in range(0, s, kc):
            bias = bias_ref[c:c + kc, :]
            for h in range(n_heads):
                sc = jnp.dot(kb_ref[h, c:c + kc, :], qt_ref[h * HEAD_DIM:(h + 1) * HEAD_DIM, :],
                             preferred_element_type=F32) + bias
                m_c = _reduce(sc, "max", 0)
                m_new = m_c if c == 0 else jnp.maximum(m[h], m_c)
                m_safe = jnp.where(m_new == NEG_INF, 0.0, m_new)
                p = jnp.exp(sc - m_safe)
                l_c = _reduce(p, "sum", 0)
                pv = jnp.dot(vt_ref[h, :, c:c + kc], p.astype(BF16), preferred_element_type=F32)
                if c == 0:
                    l[h], acc[h] = l_c, pv
                else:
                    alpha = jnp.exp(m[h] - m_safe)
                    l[h], acc[h] = alpha * l[h] + l_c, alpha * acc[h] + pv
                m[h] = m_new
        o_ref[...] = jnp.concatenate([acc[h] / l[h] for h in range(n_heads)], axis=0).T

    blocks_per_group = (t // Q_BLOCK) // n_groups
    for g in range(n_groups):
        pl.when(i // blocks_per_group == g)(
            functools.partial(body, (g + 1) * blocks_per_group * Q_BLOCK))


def _prompt_attention(q_t, qi_t, kw, k_t, v_t, batch, t, n_heads, k_sel):
    aw = n_heads * HEAD_DIM
    nq = t // Q_BLOCK
    n_groups = max(g for g in (1, 2, 4, 8) if nq % g == 0)
    qspec = lambda width: pl.BlockSpec((Q_BLOCK, width), lambda b, i: (b * nq + i, 0))
    tspec = lambda rows: pl.BlockSpec((rows, Q_BLOCK), lambda b, i: (0, b * nq + i))
    sspec = lambda width: pl.BlockSpec((t, width), lambda b, i: (b, 0))
    seq_t = pl.BlockSpec((1, aw, t), lambda b, i: (b, 0, 0))
    return pl.pallas_call(
        functools.partial(_prompt_attn_kernel, n_heads=n_heads, k_sel=k_sel, n_groups=n_groups),
        grid=(batch, nq),
        in_specs=[tspec(aw), tspec(N_IDX_HEADS * IDX_DIM), qspec(LANES),
                  seq_t, seq_t, sspec(LANES)],
        out_specs=qspec(aw),
        out_shape=jax.ShapeDtypeStruct((batch * t, aw), F32),
        scratch_shapes=[pltpu.VMEM((n_heads, t, HEAD_DIM), BF16),
                        pltpu.VMEM((n_heads, HEAD_DIM, t), BF16),
                        pltpu.VMEM((t, IDX_DIM), BF16),
                        pltpu.VMEM((t, Q_BLOCK), F32),
                        pltpu.VMEM((t, Q_BLOCK), F32)],
        compiler_params=_cparams("arbitrary", "arbitrary"),
        name="prompt_attn",
    )(q_t, qi_t, kw, k_t, v_t, kw)


SCAN_LANES = 512


def _ssm_readout(hr, hi, cwr_ref, cwi_ref):
    n_slab, kdim, _ = cwr_ref.shape
    hr, hi = hr.astype(BF16), hi.astype(BF16)
    return jnp.concatenate(
        [jnp.dot(hr[:, s * kdim:(s + 1) * kdim], cwr_ref[s], preferred_element_type=F32)
         - jnp.dot(hi[:, s * kdim:(s + 1) * kdim], cwi_ref[s], preferred_element_type=F32)
         for s in range(n_slab)], axis=1)


def _ssm_prompt_kernel(u_ref, h0r_ref, h0i_ref, bwr_ref, bwi_ref, cwr_ref, cwi_ref, d_ref, tab_ref,
                       y_ref, hr_ref, hi_ref, br_scr, bi_scr, cr_scr, ci_scr):
    c = pl.program_id(1)
    tc = u_ref.shape[0]
    gn = br_scr.shape[1]

    @pl.when(c == 0)
    def _():
        cr_scr[...] = h0r_ref[0]
        ci_scr[...] = h0i_ref[0]

    u = u_ref[...]
    ub = u.astype(BF16)
    n_in = bwr_ref.shape[0]
    kw = bwr_ref.shape[1]
    nw = bwr_ref.shape[2]
    for j in range(n_in):
        uj = ub[:, j * kw:(j + 1) * kw]
        br_scr[:, j * nw:(j + 1) * nw] = jnp.dot(uj, bwr_ref[j], preferred_element_type=F32)
        bi_scr[:, j * nw:(j + 1) * nw] = jnp.dot(uj, bwi_ref[j], preferred_element_type=F32)

    for lo in range(0, gn, SCAN_LANES):
        sl = slice(lo, lo + SCAN_LANES)
        a1r, a1i = tab_ref[0, :, sl], tab_ref[1, :, sl]
        a2r, a2i = tab_ref[2, :, sl], tab_ref[3, :, sl]
        a4r, a4i = tab_ref[4, :, sl], tab_ref[5, :, sl]
        pr, pi = tab_ref[6, :, sl], tab_ref[7, :, sl]

        def tile(j, carry):
            cr, ci = carry
            rows = pl.ds(pl.multiple_of(j * SUBLANES, SUBLANES), SUBLANES)
            br, bi = br_scr[rows, sl], bi_scr[rows, sl]
            for shift, ar, ai in ((1, a1r, a1i), (2, a2r, a2i), (4, a4r, a4i)):
                sr = pltpu.roll(br, shift, axis=0)
                si = pltpu.roll(bi, shift, axis=0)
                br, bi = br + (ar * sr - ai * si), bi + (ar * si + ai * sr)
            hr = br + (pr * cr - pi * ci)
            hi = bi + (pr * ci + pi * cr)
            br_scr[rows, sl] = hr
            bi_scr[rows, sl] = hi
            return hr[SUBLANES - 1:SUBLANES, :], hi[SUBLANES - 1:SUBLANES, :]

        cr, ci = lax.fori_loop(0, tc // SUBLANES, tile, (cr_scr[:, sl], ci_scr[:, sl]))
        cr_scr[:, sl] = cr
        ci_scr[:, sl] = ci

    y_ref[...] = _ssm_readout(br_scr[...], bi_scr[...], cwr_ref, cwi_ref) + d_ref[...] * u
    hr_ref[0] = cr_scr[...]
    hi_ref[0] = ci_scr[...]


def _ssm_prompt(u, h0r, h0i, ssm_w, batch, t, tc):
    bwr, bwi, cwr, cwi, d_skip, tab = ssm_w
    sw = u.shape[1]
    gn = cwr.shape[0] * cwr.shape[1]
    nchunk = t // tc
    full = lambda a: pl.BlockSpec(a.shape, lambda b, c: (0,) * a.ndim)
    st_spec = pl.BlockSpec((1, 1, gn), lambda b, c: (b, 0, 0))
    u_spec = pl.BlockSpec((tc, sw), lambda b, c: (b * nchunk + c, 0))
    return pl.pallas_call(
        _ssm_prompt_kernel,
        grid=(batch, nchunk),
        in_specs=[u_spec, st_spec, st_spec, full(bwr), full(bwi), full(cwr), full(cwi),
                  full(d_skip), full(tab)],
        out_specs=(u_spec, st_spec, st_spec),
        out_shape=(jax.ShapeDtypeStruct((batch * t, sw), F32),
                   jax.ShapeDtypeStruct((batch, 1, gn), F32),
                   jax.ShapeDtypeStruct((batch, 1, gn), F32)),
        scratch_shapes=[pltpu.VMEM((tc, gn), F32), pltpu.VMEM((tc, gn), F32),
                        pltpu.VMEM((1, gn), F32), pltpu.VMEM((1, gn), F32)],
        compiler_params=_cparams("arbitrary", "arbitrary"),
        name="ssm_prompt",
    )(u, h0r, h0i, bwr, bwi, cwr, cwi, d_skip, tab)


def _ssm_sample_kernel(u_ref, h0r_ref, h0i_ref, bwr_ref, bwi_ref, cwr_ref, cwi_ref, d_ref, tab_ref,
                       y_ref, hr_ref, hi_ref, *, steps):
    sw = d_ref.shape[1]
    n_in, kw, nw = bwr_ref.shape
    ar, ai = tab_ref[6, 0:1, :], tab_ref[7, 0:1, :]
    hr, hi = h0r_ref[...], h0i_ref[...]
    for s in range(steps):
        u = u_ref[:, s * sw:(s + 1) * sw]
        ub = u.astype(BF16)
        br = jnp.concatenate([jnp.dot(ub[:, j * kw:(j + 1) * kw], bwr_ref[j], preferred_element_type=F32)
                              for j in range(n_in)], axis=1)
        bi = jnp.concatenate([jnp.dot(ub[:, j * kw:(j + 1) * kw], bwi_ref[j], preferred_element_type=F32)
                              for j in range(n_in)], axis=1)
        hr, hi = (ar * hr - ai * hi) + br, (ar * hi + ai * hr) + bi
        y_ref[:, s * sw:(s + 1) * sw] = _ssm_readout(hr, hi, cwr_ref, cwi_ref) + d_ref[...] * u
    hr_ref[...] = hr
    hi_ref[...] = hi


def _ssm_sample(u2, h0r, h0i, ssm_w, steps):
    bwr, bwi, cwr, cwi, d_skip, tab = ssm_w
    nseq = u2.shape[0]
    gn = cwr.shape[0] * cwr.shape[1]
    return pl.pallas_call(
        functools.partial(_ssm_sample_kernel, steps=steps),
        out_shape=(jax.ShapeDtypeStruct(u2.shape, F32),
                   jax.ShapeDtypeStruct((nseq, gn), F32),
                   jax.ShapeDtypeStruct((nseq, gn), F32)),
        compiler_params=pltpu.CompilerParams(vmem_limit_bytes=VMEM_LIMIT),
        name="ssm_sample",
    )(u2, h0r, h0i, bwr, bwi, cwr, cwi, d_skip, tab)


def _finish_kernel(x_ref, at_ref, ga_ref, ys_ref, gs_ref, wglu_ref, bglu_ref, woa_ref, wos_ref,
                   lng_ref, lnb_ref, o_ref, *, alpha):
    a = at_ref[...] * jax.nn.silu(ga_ref[...])
    g = jax.nn.gelu(ys_ref[...], approximate=True)
    z = jnp.dot(g.astype(BF16), wglu_ref[...], preferred_element_type=F32) + bglu_ref[...]
    s = g * jax.nn.sigmoid(z) * jax.nn.silu(gs_ref[...])
    mix = (jnp.dot(a.astype(BF16), woa_ref[...], preferred_element_type=F32)
           + jnp.dot(s.astype(BF16), wos_ref[...], preferred_element_type=F32))
    r = alpha * x_ref[...] + mix
    mean = jnp.mean(r, axis=-1, keepdims=True)
    cen = r - mean
    var = jnp.mean(cen * cen, axis=-1, keepdims=True)
    o_ref[...] = cen * lax.rsqrt(var + LN_EPS) * lng_ref[...] + lnb_ref[...]


def _finish(x2, attn, g_att, y_ssm, g_ssm, fin_w, tm, alpha):
    rows, d = x2.shape
    row_spec = lambda a: pl.BlockSpec((tm, a.shape[1]), lambda i: (i, 0))
    full = lambda a: pl.BlockSpec(a.shape, lambda i: (0,) * a.ndim)
    acts = (x2, attn, g_att, y_ssm, g_ssm)
    return pl.pallas_call(
        functools.partial(_finish_kernel, alpha=alpha),
        grid=(rows // tm,),
        in_specs=[row_spec(a) for a in acts] + [full(w) for w in fin_w],
        out_specs=pl.BlockSpec((tm, d), lambda i: (i, 0)),
        out_shape=jax.ShapeDtypeStruct((rows, d), F32),
        compiler_params=_cparams("arbitrary"),
        name="finish",
    )(*acts, *fin_w)


def _page_copy(pt_ref, src_hbm, buf, sem, seq, first_page, slot, buf_off, j):
    return pltpu.make_async_copy(src_hbm.at[pt_ref[seq, first_page + j]], buf.at[slot, buf_off + j],
                                 sem.at[slot])


def _start_pages(n, *copy_args):
    def go(j, carry):
        _page_copy(*copy_args, j).start()
        return carry
    lax.fori_loop(0, n, go, 0)


def _wait_pages(n, *copy_args):
    def go(j, carry):
        _page_copy(*copy_args, j).wait()
        return carry
    lax.fori_loop(0, n, go, 0)


def _sample_index_kernel(pt_ref, qi_ref, wi_ref, kin_ref, kidx_hbm, bias_ref, buf, sem,
                         *, n_pages, page, steps, k_sel, group):
    s = pl.program_id(0)
    nsteps = pl.num_programs(0)
    slot = s % 2

    def for_group(fn, step, sl):
        for g in range(group):
            fn(n_pages, pt_ref, kidx_hbm, buf, sem, step * group + g, 0, sl, g * n_pages)

    @pl.when(s == 0)
    def _():
        for_group(_start_pages, 0, 0)

    @pl.when(s + 1 < nsteps)
    def _():
        for_group(_start_pages, s + 1, 1 - slot)

    for_group(_wait_pages, s, slot)

    past = n_pages * page
    nt = (((1,), (1,)), ((), ()))
    w_scale = (IDX_DIM ** -0.5) * (N_IDX_HEADS ** -0.5)
    pad = jnp.zeros((LANES - SUBLANES, IDX_DIM), F32)

    def head_sum(d, wi):
        w = jnp.maximum(d, 0.0) * wi
        out = w[0:SUBLANES]
        for h in range(1, N_IDX_HEADS):
            out = out + w[h * SUBLANES:(h + 1) * SUBLANES]
        return out

    scores = []
    for g in range(group):
        qi = qi_ref[g]
        wi = wi_ref[g] * w_scale
        ki_t = jnp.concatenate([buf[slot, g * n_pages + j].astype(BF16) for j in range(n_pages)],
                               axis=1)
        ki_new = jnp.concatenate([kin_ref[g][:, :IDX_DIM], pad], axis=0).astype(BF16)
        sc_past = head_sum(jnp.dot(qi, ki_t, preferred_element_type=F32), wi)
        sc_new = head_sum(lax.dot_general(qi, ki_new, nt, preferred_element_type=F32), wi)
        scores.append(jnp.concatenate([sc_past, sc_new], axis=1))
    score = jnp.concatenate(scores, axis=0)
    shape = score.shape
    kpos = lax.broadcasted_iota(I32, shape, 1)
    qstep = lax.broadcasted_iota(I32, shape, 0) & (SUBLANES - 1)
    valid = kpos <= jnp.minimum(past + qstep, past + steps - 1)
    score = jnp.where(valid, score, NEG_INF)
    score = jnp.where(qstep < steps, score, -kpos.astype(F32))
    bias = jnp.where(valid, _topk_select(score, kpos, k_sel, axis=1), NEG_INF)
    for g in range(group):
        bias_ref[g] = bias[g * SUBLANES:(g + 1) * SUBLANES]


def _sample_index(page_table, qi_hs, wi_hs, kw_new, kidx_pages, steps, k_sel, group):
    nseq, n_pages = page_table.shape
    page = kidx_pages.shape[2]
    past = n_pages * page
    blk = lambda a: pl.BlockSpec((group,) + a.shape[1:], lambda s, pt: (s, 0, 0))
    grid_spec = pltpu.PrefetchScalarGridSpec(
        num_scalar_prefetch=1,
        grid=(nseq // group,),
        in_specs=[blk(qi_hs), blk(wi_hs), blk(kw_new), pl.BlockSpec(memory_space=pl.ANY)],
        out_specs=pl.BlockSpec((group, SUBLANES, past + LANES), lambda s, pt: (s, 0, 0)),
        scratch_shapes=[pltpu.VMEM((2, group * n_pages, IDX_DIM, page), F32),
                        pltpu.SemaphoreType.DMA((2,))],
    )
    return pl.pallas_call(
        functools.partial(_sample_index_kernel, n_pages=n_pages, page=page, steps=steps, k_sel=k_sel,
                          group=group),
        grid_spec=grid_spec,
        out_shape=jax.ShapeDtypeStruct((nseq, SUBLANES, past + LANES), F32),
        compiler_params=_cparams("arbitrary"),
        name="sample_index",
    )(page_table, qi_hs, wi_hs, kw_new, kidx_pages)


def _sample_attn_kernel(pt_ref, q_ref, bias_ref, biasn_ref, kn_ref, vn_ref, k_hbm, v_hbm, o_ref,
                        kbuf, vbuf, ksem, vsem, m_scr, l_scr, acc_scr,
                        *, n_chunks, cpages, page, n_heads):
    step = pl.program_id(0)
    nsteps = pl.num_programs(0)
    slot = step % 2

    def both(fn, item, sl):
        seq, ch = item // n_chunks, item % n_chunks
        fn(cpages, pt_ref, k_hbm, kbuf, ksem, seq, ch * cpages, sl, 0)
        fn(cpages, pt_ref, v_hbm, vbuf, vsem, seq, ch * cpages, sl, 0)

    @pl.when(step == 0)
    def _():
        both(_start_pages, step, 0)

    @pl.when(step + 1 < nsteps)
    def _():
        both(_start_pages, step + 1, 1 - slot)

    both(_wait_pages, step, slot)
    ch = step % n_chunks

    @pl.when(ch == 0)
    def _():
        m_scr[...] = jnp.full(m_scr.shape, NEG_INF, F32)
        l_scr[...] = jnp.zeros(l_scr.shape, F32)
        acc_scr[...] = jnp.zeros(acc_scr.shape, F32)

    nt = (((1,), (1,)), ((), ()))

    def update(k_t, v_t, bias8):
        sc = jnp.concatenate([jnp.dot(q_ref[0, h], k_t[h], preferred_element_type=F32)
                              for h in range(n_heads)], axis=0)
        sc = sc + jnp.concatenate([bias8] * n_heads, axis=0)
        m_old = m_scr[...]
        m_new = jnp.maximum(m_old, jnp.max(sc, axis=1, keepdims=True))
        m_safe = jnp.where(m_new == NEG_INF, 0.0, m_new)
        alpha = jnp.exp(m_old - m_safe)
        p = jnp.exp(sc - m_safe)
        l_scr[...] = alpha * l_scr[...] + jnp.sum(p, axis=1, keepdims=True)
        pv = jnp.concatenate([lax.dot_general(p[h * SUBLANES:(h + 1) * SUBLANES].astype(BF16), v_t[h], nt,
                                              preferred_element_type=F32)
                              for h in range(n_heads)], axis=0)
        acc_scr[...] = alpha * acc_scr[...] + pv
        m_scr[...] = m_new

    def chunk_t(buf, h):
        return jnp.concatenate([buf[slot, j, h].astype(BF16) for j in range(cpages)], axis=1)

    update([chunk_t(kbuf, h) for h in range(n_heads)], [chunk_t(vbuf, h) for h in range(n_heads)],
           bias_ref[0])

    @pl.when(ch == n_chunks - 1)
    def _():
        pad = jnp.zeros((LANES - SUBLANES, n_heads * HEAD_DIM), F32)

        def new_t(ref):
            x_t = jnp.concatenate([ref[0], pad], axis=0).T
            return [x_t[h * HEAD_DIM:(h + 1) * HEAD_DIM].astype(BF16) for h in range(n_heads)]

        update(new_t(kn_ref), new_t(vn_ref), biasn_ref[0])
        o = acc_scr[...] / l_scr[...]
        o_ref[0] = jnp.concatenate([o[h * SUBLANES:(h + 1) * SUBLANES] for h in range(n_heads)], axis=1)


def _sample_attention(page_table, q_hs, bias, k_new, v_new, k_pages, v_pages, cpages):
    nseq, n_pages = page_table.shape
    n_heads, page = k_pages.shape[1], k_pages.shape[3]
    aw = n_heads * HEAD_DIM
    n_chunks = n_pages // cpages
    cs = cpages * page
    rows = n_heads * SUBLANES
    grid_spec = pltpu.PrefetchScalarGridSpec(
        num_scalar_prefetch=1,
        grid=(nseq * n_chunks,),
        in_specs=[pl.BlockSpec((1, n_heads, SUBLANES, HEAD_DIM), lambda i, pt: (i // n_chunks, 0, 0, 0)),
                  pl.BlockSpec((1, SUBLANES, cs), lambda i, pt: (i // n_chunks, 0, i % n_chunks)),
                  pl.BlockSpec((1, SUBLANES, LANES), lambda i, pt: (i // n_chunks, 0, n_pages * page // LANES)),
                  pl.BlockSpec((1, SUBLANES, aw), lambda i, pt: (i // n_chunks, 0, 0)),
                  pl.BlockSpec((1, SUBLANES, aw), lambda i, pt: (i // n_chunks, 0, 0)),
                  pl.BlockSpec(memory_space=pl.ANY),
                  pl.BlockSpec(memory_space=pl.ANY)],
        out_specs=pl.BlockSpec((1, SUBLANES, aw), lambda i, pt: (i // n_chunks, 0, 0)),
        scratch_shapes=[pltpu.VMEM((2, cpages, n_heads, HEAD_DIM, page), F32),
                        pltpu.VMEM((2, cpages, n_heads, HEAD_DIM, page), F32),
                        pltpu.SemaphoreType.DMA((2,)),
                        pltpu.SemaphoreType.DMA((2,)),
                        pltpu.VMEM((rows, 1), F32),
                        pltpu.VMEM((rows, 1), F32),
                        pltpu.VMEM((rows, HEAD_DIM), F32)],
    )
    return pl.pallas_call(
        functools.partial(_sample_attn_kernel, n_chunks=n_chunks, cpages=cpages, page=page,
                          n_heads=n_heads),
        grid_spec=grid_spec,
        out_shape=jax.ShapeDtypeStruct((nseq, SUBLANES, aw), F32),
        compiler_params=_cparams("arbitrary"),
        name="sample_attn",
    )(page_table, q_hs, bias, bias, k_new, v_new, k_pages, v_pages)


def _pack_w_in(w, aw, sw):
    sizes = [aw] * 4 + [N_IDX_HEADS * IDX_DIM, IDX_DIM, N_IDX_HEADS, sw, sw]
    offs = np.cumsum([0] + sizes)
    seg = [w[:, offs[j]:offs[j + 1]] for j in range(len(sizes))]
    pad = jnp.zeros((w.shape[0], LANES - IDX_DIM - N_IDX_HEADS), w.dtype)
    return jnp.concatenate(seg[:5] + [seg[5], seg[6], pad] + seg[7:], axis=1).astype(BF16)


def _ssm_weights(a_re, a_im, log_dt, b_re, b_im, c_re, c_im, d_skip):
    g, n, c = b_re.shape
    a_re, a_im = a_re.astype(F32), a_im.astype(F32)
    dt = jnp.exp(log_dt.astype(F32))[:, None]
    mag = jnp.exp(dt * a_re)
    abr = mag * jnp.cos(dt * a_im)
    abi = mag * jnp.sin(dt * a_im)
    den = a_re * a_re + a_im * a_im
    nr = abr - 1.0
    cr = (nr * a_re + abi * a_im) / den
    ci = (abi * a_re - nr * a_im) / den
    b_re, b_im = b_re.astype(F32), b_im.astype(F32)
    bbr = cr[..., None] * b_re - ci[..., None] * b_im
    bbi = cr[..., None] * b_im + ci[..., None] * b_re

    kw_lanes = 2 * LANES
    gpk = kw_lanes // c
    n_in = g // gpk
    eye = jnp.eye(gpk, dtype=F32)

    def in_map(bb):
        bb = bb.reshape(n_in, gpk, n, c)
        return jnp.einsum("jgnc,gh->jgchn", bb, eye).reshape(n_in, gpk * c, gpk * n).astype(BF16)

    gps = LANES // c

    def out_map(cc):
        cc = cc.astype(F32).reshape(g // gps, gps, c, n)
        return jnp.einsum("sgcn,gh->sgnhc", cc, jnp.eye(gps, dtype=F32)).reshape(
            g // gps, gps * n, gps * c).astype(BF16)

    ar, ai = abr.reshape(1, g * n), abi.reshape(1, g * n)
    pows_r, pows_i = [ar], [ai]
    for _ in range(SUBLANES - 1):
        pr, pi = pows_r[-1], pows_i[-1]
        pows_r.append(pr * ar - pi * ai)
        pows_i.append(pr * ai + pi * ar)
    row = jnp.arange(SUBLANES)[:, None]

    def masked(k):
        keep = row >= k
        return (jnp.where(keep, pows_r[k - 1], 0.0), jnp.where(keep, pows_i[k - 1], 0.0))

    m1, m2, m4 = masked(1), masked(2), masked(4)
    tab = jnp.stack([m1[0], m1[1], m2[0], m2[1], m4[0], m4[1],
                     jnp.concatenate(pows_r, axis=0), jnp.concatenate(pows_i, axis=0)], axis=0)
    return (in_map(bbr), in_map(bbi), out_map(c_re), out_map(c_im),
            d_skip.astype(F32).reshape(1, g * c), tab)


def kernel(x_prompt, x_sample, cache_k, cache_v, cache_kidx, page_table, state_ssm_re, state_ssm_im,
           w_in, w_out, ssm_a_re, ssm_a_im, ssm_log_dt, ssm_b_re, ssm_b_im, ssm_c_re, ssm_c_im,
           ssm_d, w_glu, b_glu, ln_g, ln_b):
    depth = w_in.shape[0]
    assert depth == 1, "single-layer trunk"
    batch, t, d = x_prompt.shape
    nseq, steps, _ = x_sample.shape
    n_pool, page, n_heads, head_dim = cache_k.shape[1:]
    assert head_dim == HEAD_DIM and steps <= SUBLANES
    aw = n_heads * HEAD_DIM
    g_ssm_n = ssm_b_re.shape[1]
    sw = g_ssm_n * SSM_GROUP
    gn = g_ssm_n * SSM_STATE
    n_pages = page_table.shape[1]
    past = n_pages * page
    alpha = (2.0 * depth) ** 0.25

    layer = 0
    w_packed = _pack_w_in(w_in[layer], aw, sw)
    ssm_w = _ssm_weights(ssm_a_re[layer], ssm_a_im[layer], ssm_log_dt[layer], ssm_b_re[layer],
                         ssm_b_im[layer], ssm_c_re[layer], ssm_c_im[layer], ssm_d[layer])
    fin_w = (w_glu[layer].astype(BF16), b_glu[layer].astype(F32).reshape(1, sw),
             w_out[layer][:aw].astype(BF16), w_out[layer][aw:].astype(BF16),
             ln_g[layer].astype(F32).reshape(1, d), ln_b[layer].astype(F32).reshape(1, d))

    tm = 512
    xp2 = x_prompt.reshape(batch * t, d)
    q, k_t, v_t, g_att, qi, kw, ki_t, u, g_ssm = _project(
        xp2, w_packed, _rope_tables(jnp.arange(t)), tm, batch, aw, sw)
    attn = _prompt_attention(q, qi, kw, k_t, v_t, batch, t, n_heads, min(TOPK_MAX, t // 4))
    zeros_state = jnp.zeros((batch, 1, gn), F32)
    y_ssm, hr_p, hi_p = _ssm_prompt(u, zeros_state, zeros_state, ssm_w, batch, t, min(512, t))
    y_prompt = _finish(xp2, attn, g_att, y_ssm, g_ssm, fin_w, tm, alpha).reshape(batch, t, d)
    to_cache = lambda a_t: jnp.transpose(a_t.reshape(1, batch, n_heads, HEAD_DIM, t), (0, 1, 4, 2, 3))
    new_k_p = to_cache(k_t)
    new_v_p = to_cache(v_t)
    new_ki_p = jnp.transpose(ki_t, (0, 2, 1))[None]
    new_hr_p = hr_p.reshape(1, batch, g_ssm_n, SSM_STATE)
    new_hi_p = hi_p.reshape(1, batch, g_ssm_n, SSM_STATE)

    rows_s = nseq * steps
    xs2 = x_sample.reshape(rows_s, d)
    pos_s = past + (jnp.arange(rows_s) % steps)
    qs, ks_t, vs_t, gas, qis, kws, _, us, gss = _project(
        xs2, w_packed, _rope_tables(pos_s), rows_s, 1, aw, sw)
    ks, vs = ks_t[0].T, vs_t[0].T

    def head_rows(a, width):
        a = a.reshape(nseq, steps, -1, width).transpose(0, 2, 1, 3)
        return jnp.pad(a, ((0, 0), (0, 0), (0, SUBLANES - steps), (0, 0)))

    def step_rows(a):
        return jnp.pad(a.reshape(nseq, steps, -1), ((0, 0), (0, SUBLANES - steps), (0, 0)))

    qi_hs = head_rows(qis.T, IDX_DIM).reshape(nseq, N_IDX_HEADS * SUBLANES, IDX_DIM)
    wi_hs = head_rows(kws[:, IDX_DIM:IDX_DIM + N_IDX_HEADS], 1).reshape(nseq, N_IDX_HEADS * SUBLANES, 1)
    q_hs = head_rows(qs.T, HEAD_DIM)

    kidx_t = jnp.transpose(cache_kidx[layer], (0, 2, 1))
    ck_t = jnp.transpose(cache_k[layer], (0, 2, 3, 1))
    cv_t = jnp.transpose(cache_v[layer], (0, 2, 3, 1))

    k_sel_s = min(TOPK_MAX, (past + steps) // 4)
    group = 4 if nseq % 4 == 0 else 1
    bias = _sample_index(page_table, qi_hs, wi_hs, step_rows(kws), kidx_t, steps, k_sel_s, group)
    cpages = 16 if n_pages % 16 == 0 else n_pages
    attn_s = _sample_attention(page_table, q_hs, bias, step_rows(ks), step_rows(vs), ck_t, cv_t, cpages)
    attn_s = attn_s[:, :steps].reshape(rows_s, aw)

    y_s, hr_s, hi_s = _ssm_sample(us.reshape(nseq, steps * sw),
                                  state_ssm_re[layer].reshape(nseq, gn).astype(F32),
                                  state_ssm_im[layer].reshape(nseq, gn).astype(F32), ssm_w, steps)
    y_sample = _finish(xs2, attn_s, gas, y_s.reshape(rows_s, sw), gss, fin_w, rows_s, alpha)
    y_sample = y_sample.reshape(nseq, steps, d)

    return (y_prompt, y_sample, new_k_p, new_v_p, new_ki_p, new_hr_p, new_hi_p,
            ks.reshape(1, nseq, steps, n_heads, HEAD_DIM), vs.reshape(1, nseq, steps, n_heads, HEAD_DIM),
            kws[:, :IDX_DIM].reshape(1, nseq, steps, IDX_DIM),
            hr_s.reshape(1, nseq, g_ssm_n, SSM_STATE), hi_s.reshape(1, nseq, g_ssm_n, SSM_STATE))
```

```python
import functools
import math

import jax
import jax.numpy as jnp
import numpy as np
from jax import lax
from jax.experimental import pallas as pl
from jax.experimental.pallas import tpu as pltpu

F32 = jnp.float32
BF16 = jnp.bfloat16
I32 = jnp.int32

HEAD_DIM = 64
ROT_DIM = HEAD_DIM // 4
ROPE_THETA = 500000.0
N_IDX_HEADS = 8
IDX_DIM = 64
TOPK_MAX = 256
Q_BLOCK = 128
SSM_GROUP = 16
SSM_STATE = 64
LN_EPS = 1e-5

LANES = 128
SUBLANES = 8
VMEM_LIMIT = 52 * 1024 * 1024

INT_MIN = -(2 ** 31)
NEG_INF = float("-inf")


def _cparams(*sem):
    return pltpu.CompilerParams(dimension_semantics=sem, vmem_limit_bytes=VMEM_LIMIT)


def _ordered_to_float(o):
    return pltpu.bitcast(jnp.where(o < 0, o ^ jnp.int32(0x7FFFFFFF), o), F32)


REDUCE_CHAINS = 8


def _reduce(x, kind, axis):
    pair, full = (jnp.add, jnp.sum) if kind == "sum" else (jnp.maximum, jnp.max)
    n = x.shape[axis]
    unit = SUBLANES if axis == 0 else LANES
    if n % unit == 0 and n // unit > REDUCE_CHAINS:
        nv = n // unit
        chains = min((d for d in range(1, nv + 1) if nv % d == 0), key=lambda d: abs(d - REDUCE_CHAINS))
        width = unit * chains
        take = (lambda c: x[c:c + width, :]) if axis == 0 else (lambda c: x[:, c:c + width])
        acc = take(0)
        for c in range(width, n, width):
            acc = pair(acc, take(c))
        x = acc
    return full(x, axis=axis, keepdims=True)


def _topk_select(score, kpos, k_sel, axis):
    kf = float(k_sel)
    n_idx_bits = max(1, int(math.ceil(math.log2(score.shape[axis] + 1))))
    qshape = tuple(1 if a == axis else n for a, n in enumerate(score.shape))

    def count(mask):
        return _reduce(jnp.where(mask, 1.0, 0.0), "sum", axis)

    ans0 = jnp.where(count(score >= 0.0) >= kf, jnp.int32(0), jnp.int32(INT_MIN))

    def val_step(b, ans):
        cand = ans + jnp.left_shift(jnp.int32(1), jnp.int32(30) - b)
        return jnp.where(count(score >= _ordered_to_float(cand)) >= kf, cand, ans)

    enough = count(score > NEG_INF) >= kf
    thr = jnp.where(enough, _ordered_to_float(lax.fori_loop(0, 31, val_step, ans0)), NEG_INF)
    gt = score > thr
    eq = score == thr
    need = kf - count(gt)
    has_ties = jnp.max(jnp.where(enough, count(eq) - need, 0.0)) > 0.0

    def tie_search():
        def idx_step(b, j):
            cand = j + jnp.left_shift(jnp.int32(1), jnp.int32(n_idx_bits - 1) - b)
            c = count(jnp.where(eq, kpos, jnp.int32(2 ** 30)) < cand)
            return jnp.where(c < need, cand, j)
        return lax.fori_loop(0, n_idx_bits, idx_step, jnp.zeros(qshape, I32))

    def no_ties():
        return jnp.full(qshape, 2 ** 30, I32)

    jthr = lax.cond(has_ties, tie_search, no_ties)
    tie_bias = jnp.where(kpos <= jthr, 0.0, NEG_INF)
    return jnp.where(gt, 0.0, jnp.where(eq, tie_bias, NEG_INF))


def _rope(x, cosf, sin_up, sin_dn):
    half = ROT_DIM // 2
    return (x * cosf + pltpu.roll(x, LANES - half, axis=1) * sin_up
            + pltpu.roll(x, half, axis=1) * sin_dn)


def _proj_kernel(x_ref, w_ref, cos_ref, sup_ref, sdn_ref,
                 q_ref, k_ref, v_ref, ga_ref, qi_ref, kw_ref, ki_ref, u_ref, gs_ref, v_scr, *, aw, sw):
    xb = x_ref[...].astype(BF16)
    cosf, sup, sdn = cos_ref[...], sup_ref[...], sdn_ref[...]

    def seg(off, width):
        return jnp.dot(xb, w_ref[:, off:off + width], preferred_element_type=F32)

    def rope_seg(h):
        return jnp.concatenate(
            [_rope(h[:, c:c + LANES], cosf, sup, sdn) for c in range(0, h.shape[1], LANES)], axis=1)

    off = 0
    q_ref[...] = (rope_seg(seg(off, aw)) * (HEAD_DIM ** -0.5)).T.astype(BF16); off += aw
    k_ref[0] = rope_seg(seg(off, aw)).T; off += aw
    v_scr[...] = seg(off, aw); off += aw
    v_ref[0] = v_scr[...].T
    ga_ref[...] = seg(off, aw); off += aw
    qi_ref[...] = rope_seg(seg(off, N_IDX_HEADS * IDX_DIM)).T.astype(BF16); off += N_IDX_HEADS * IDX_DIM
    kw = seg(off, LANES); off += LANES
    lane = lax.broadcasted_iota(I32, kw.shape, 1)
    is_key = lane < IDX_DIM
    kw = _rope(kw, jnp.where(is_key, cosf, 1.0), jnp.where(is_key, sup, 0.0),
               jnp.where(is_key, sdn, 0.0))
    kw_ref[...] = kw
    ki_ref[0] = kw.T[:IDX_DIM, :]
    u_ref[...] = seg(off, sw); off += sw
    gs_ref[...] = seg(off, sw)


def _rope_tables(pos):
    half = ROT_DIM // 2
    inv = ROPE_THETA ** (-2.0 * jnp.arange(half, dtype=F32) / ROT_DIM)
    ang = pos.astype(F32)[:, None] * inv[None, :]
    cos, sin = jnp.cos(ang), jnp.sin(ang)
    p = pos.shape[0]
    ones = jnp.ones((p, HEAD_DIM - ROT_DIM), F32)
    zeros = jnp.zeros((p, HEAD_DIM - ROT_DIM), F32)
    zh = jnp.zeros((p, half), F32)
    cos64 = jnp.concatenate([cos, cos, ones], axis=1)
    up64 = jnp.concatenate([-sin, zh, zeros], axis=1)
    dn64 = jnp.concatenate([zh, sin, zeros], axis=1)
    rep = LANES // HEAD_DIM
    return (jnp.tile(cos64, (1, rep)), jnp.tile(up64, (1, rep)), jnp.tile(dn64, (1, rep)))


def _project(x2, w_packed, tables, tm, nseq, aw, sw):
    rows, d = x2.shape
    t = rows // nseq
    nblk = t // tm
    n_tab_blocks = tables[0].shape[0] // tm
    row_spec = lambda width: pl.BlockSpec((tm, width), lambda i: (i, 0))
    col_spec = lambda height: pl.BlockSpec((height, tm), lambda i: (0, i))
    seq_spec = lambda height: pl.BlockSpec((1, height, tm), lambda i: (i // nblk, 0, i % nblk))
    tab_spec = pl.BlockSpec((tm, LANES), lambda i: (i % n_tab_blocks, 0))
    idx_w = N_IDX_HEADS * IDX_DIM
    out_shapes = (
        jax.ShapeDtypeStruct((aw, rows), BF16),
        jax.ShapeDtypeStruct((nseq, aw, t), F32),
        jax.ShapeDtypeStruct((nseq, aw, t), F32),
        jax.ShapeDtypeStruct((rows, aw), F32),
        jax.ShapeDtypeStruct((idx_w, rows), BF16),
        jax.ShapeDtypeStruct((rows, LANES), F32),
        jax.ShapeDtypeStruct((nseq, IDX_DIM, t), F32),
        jax.ShapeDtypeStruct((rows, sw), F32),
        jax.ShapeDtypeStruct((rows, sw), F32),
    )
    out_specs = (col_spec(aw), seq_spec(aw), seq_spec(aw), row_spec(aw), col_spec(idx_w),
                 row_spec(LANES), seq_spec(IDX_DIM), row_spec(sw), row_spec(sw))
    return pl.pallas_call(
        functools.partial(_proj_kernel, aw=aw, sw=sw),
        grid=(rows // tm,),
        in_specs=[row_spec(d), pl.BlockSpec(w_packed.shape, lambda i: (0, 0)),
                  tab_spec, tab_spec, tab_spec],
        out_specs=out_specs,
        out_shape=out_shapes,
        scratch_shapes=[pltpu.VMEM((tm, aw), F32)],
        compiler_params=_cparams("arbitrary"),
        name="in_proj",
    )(x2, w_packed, *tables)


KEY_CHUNK = 2048
HEADS_PER_DOT = LANES // HEAD_DIM
IDX_HEADS_PER_DOT = 4


def _prompt_attn_kernel(qt_ref, qit_ref, kwq_ref, k_ref, v_ref, kw_ref, o_ref,
                        kb_ref, vt_ref, kib_ref, score_ref, bias_ref, *, n_heads, k_sel, n_groups):
    i = pl.program_id(1)
    t = kw_ref.shape[0]

    @pl.when(i == 0)
    def _():
        for h in range(n_heads):
            vt_ref[h] = v_ref[0, h * HEAD_DIM:(h + 1) * HEAD_DIM, :].astype(BF16)
        for c in range(0, t, LANES):
            kc_rows = k_ref[0, :, c:c + LANES].T
            for p in range(n_heads // HEADS_PER_DOT):
                kb_ref[p, c:c + LANES, :] = kc_rows[:, p * LANES:(p + 1) * LANES].astype(BF16)
        kib_ref[...] = kw_ref[:, :IDX_DIM].astype(BF16)

    w_scale = (IDX_DIM ** -0.5) * (N_IDX_HEADS ** -0.5)

    def head_pair_rhs(p):
        qa = qt_ref[(2 * p) * HEAD_DIM:(2 * p + 1) * HEAD_DIM, :]
        qb = qt_ref[(2 * p + 1) * HEAD_DIM:(2 * p + 2) * HEAD_DIM, :]
        z = jnp.zeros_like(qa)
        return jnp.concatenate([jnp.concatenate([qa, z], axis=1), jnp.concatenate([z, qb], axis=1)], axis=0)

    def body(s):
        kc = max(d for d in range(LANES, KEY_CHUNK + 1, LANES) if s % d == 0)
        w_t = kwq_ref[...].T[IDX_DIM:IDX_DIM + N_IDX_HEADS, :] * w_scale
        qpos = i * Q_BLOCK + lax.broadcasted_iota(I32, (kc, Q_BLOCK), 1)
        row = lax.broadcasted_iota(I32, (kc, Q_BLOCK), 0)
        idx_rhs = [jnp.concatenate([qit_ref[h * IDX_DIM:(h + 1) * IDX_DIM, :]
                                    for h in range(g, g + IDX_HEADS_PER_DOT)], axis=1)
                   for g in range(0, N_IDX_HEADS, IDX_HEADS_PER_DOT)]
        for c in range(0, s, kc):
            kib = kib_ref[c:c + kc, :]
            score = None
            for g, rhs in enumerate(idx_rhs):
                d = jnp.dot(kib, rhs, preferred_element_type=F32)
                for j in range(IDX_HEADS_PER_DOT):
                    h = g * IDX_HEADS_PER_DOT + j
                    term = jnp.maximum(d[:, j * Q_BLOCK:(j + 1) * Q_BLOCK], 0.0) * w_t[h:h + 1, :]
                    score = term if score is None else score + term
            score_ref[c:c + kc, :] = jnp.where(c + row <= qpos, score, NEG_INF)

        kpos = lax.broadcasted_iota(I32, (s, Q_BLOCK), 0)
        valid = kpos <= i * Q_BLOCK + lax.broadcasted_iota(I32, (s, Q_BLOCK), 1)
        bias_ref[0:s, :] = jnp.where(valid, _topk_select(score_ref[0:s, :], kpos, k_sel, axis=0), NEG_INF)

        m, l, acc = [None] * n_heads, [None] * n_heads, [None] * n_heads
        pair_rhs = [head_pair_rhs(p) for p in range(n_heads // HEADS_PER_DOT)]
        for c in range(0, s, kc):
            bias = bias_ref[c:c + kc, :]
            for h in range(n_heads):
                if h % HEADS_PER_DOT == 0:
                    sc_pair = jnp.dot(kb_ref[h // HEADS_PER_DOT, c:c + kc, :], pair_rhs[h // HEADS_PER_DOT],
                                      preferred_element_type=F32)
                j = h % HEADS_PER_DOT
                sc = sc_pair[:, j * Q_BLOCK:(j + 1) * Q_BLOCK] + bias
                m_c = _reduce(sc, "max", 0)
                m_new = m_c if c == 0 else jnp.maximum(m[h], m_c)
                m_safe = jnp.where(m_new == NEG_INF, 0.0, m_new)
                p = jnp.exp(sc - m_safe)
                l_c = _reduce(p, "sum", 0)
                pv = jnp.dot(vt_ref[h, :, c:c + kc], p.astype(BF16), preferred_element_type=F32)
                if c == 0:
                    l[h], acc[h] = l_c, pv
                else:
                    alpha = jnp.exp(m[h] - m_safe)
                    l[h], acc[h] = alpha * l[h] + l_c, alpha * acc[h] + pv
                m[h] = m_new
        o_ref[...] = jnp.concatenate([acc[h] / l[h] for h in range(n_heads)], axis=0).T

    blocks_per_group = (t // Q_BLOCK) // n_groups
    for g in range(n_groups):
        pl.when(i // blocks_per_group == g)(
            functools.partial(body, (g + 1) * blocks_per_group * Q_BLOCK))


def _prompt_attention(q_t, qi_t, kw, k_t, v_t, batch, t, n_heads, k_sel):
    aw = n_heads * HEAD_DIM
    nq = t // Q_BLOCK
    n_groups = max(g for g in (1, 2, 4, 8) if nq % g == 0)
    qspec = lambda width: pl.BlockSpec((Q_BLOCK, width), lambda b, i: (b * nq + i, 0))
    tspec = lambda rows: pl.BlockSpec((rows, Q_BLOCK), lambda b, i: (0, b * nq + i))
    sspec = lambda width: pl.BlockSpec((t, width), lambda b, i: (b, 0))
    seq_t = pl.BlockSpec((1, aw, t), lambda b, i: (b, 0, 0))
    return pl.pallas_call(
        functools.partial(_prompt_attn_kernel, n_heads=n_heads, k_sel=k_sel, n_groups=n_groups),
        grid=(batch, nq),
        in_specs=[tspec(aw), tspec(N_IDX_HEADS * IDX_DIM), qspec(LANES),
                  seq_t, seq_t, sspec(LANES)],
        out_specs=qspec(aw),
        out_shape=jax.ShapeDtypeStruct((batch * t, aw), F32),
        scratch_shapes=[pltpu.VMEM((n_heads // HEADS_PER_DOT, t, LANES), BF16),
                        pltpu.VMEM((n_heads, HEAD_DIM, t), BF16),
                        pltpu.VMEM((t, IDX_DIM), BF16),
                        pltpu.VMEM((t, Q_BLOCK), F32),
                        pltpu.VMEM((t, Q_BLOCK), F32)],
        compiler_params=_cparams("arbitrary", "arbitrary"),
        name="prompt_attn",
    )(q_t, qi_t, kw, k_t, v_t, kw)


SCAN_LANES = 512


def _ssm_readout(hr, hi, cwr_ref, cwi_ref):
    n_slab, kdim, _ = cwr_ref.shape
    hr, hi = hr.astype(BF16), hi.astype(BF16)
    return jnp.concatenate(
        [jnp.dot(hr[:, s * kdim:(s + 1) * kdim], cwr_ref[s], preferred_element_type=F32)
         - jnp.dot(hi[:, s * kdim:(s + 1) * kdim], cwi_ref[s], preferred_element_type=F32)
         for s in range(n_slab)], axis=1)


def _ssm_prompt_kernel(u_ref, h0r_ref, h0i_ref, bwr_ref, bwi_ref, cwr_ref, cwi_ref, d_ref, tab_ref,
                       y_ref, hr_ref, hi_ref, br_scr, bi_scr, cr_scr, ci_scr):
    c = pl.program_id(1)
    tc = u_ref.shape[0]
    gn = br_scr.shape[1]

    @pl.when(c == 0)
    def _():
        cr_scr[...] = h0r_ref[0]
        ci_scr[...] = h0i_ref[0]

    u = u_ref[...]
    ub = u.astype(BF16)
    n_in = bwr_ref.shape[0]
    kw = bwr_ref.shape[1]
    nw = bwr_ref.shape[2]
    for j in range(n_in):
        uj = ub[:, j * kw:(j + 1) * kw]
        br_scr[:, j * nw:(j + 1) * nw] = jnp.dot(uj, bwr_ref[j], preferred_element_type=F32)
        bi_scr[:, j * nw:(j + 1) * nw] = jnp.dot(uj, bwi_ref[j], preferred_element_type=F32)

    for lo in range(0, gn, SCAN_LANES):
        sl = slice(lo, lo + SCAN_LANES)
        a1r, a1i = tab_ref[0, :, sl], tab_ref[1, :, sl]
        a2r, a2i = tab_ref[2, :, sl], tab_ref[3, :, sl]
        a4r, a4i = tab_ref[4, :, sl], tab_ref[5, :, sl]
        pr, pi = tab_ref[6, :, sl], tab_ref[7, :, sl]

        def tile(j, carry):
            cr, ci = carry
            rows = pl.ds(pl.multiple_of(j * SUBLANES, SUBLANES), SUBLANES)
            br, bi = br_scr[rows, sl], bi_scr[rows, sl]
            for shift, ar, ai in ((1, a1r, a1i), (2, a2r, a2i), (4, a4r, a4i)):
                sr = pltpu.roll(br, shift, axis=0)
                si = pltpu.roll(bi, shift, axis=0)
                br, bi = br + (ar * sr - ai * si), bi + (ar * si + ai * sr)
            hr = br + (pr * cr - pi * ci)
            hi = bi + (pr * ci + pi * cr)
            br_scr[rows, sl] = hr
            bi_scr[rows, sl] = hi
            return hr[SUBLANES - 1:SUBLANES, :], hi[SUBLANES - 1:SUBLANES, :]

        cr, ci = lax.fori_loop(0, tc // SUBLANES, tile, (cr_scr[:, sl], ci_scr[:, sl]))
        cr_scr[:, sl] = cr
        ci_scr[:, sl] = ci

    y_ref[...] = _ssm_readout(br_scr[...], bi_scr[...], cwr_ref, cwi_ref) + d_ref[...] * u
    hr_ref[0] = cr_scr[...]
    hi_ref[0] = ci_scr[...]


def _ssm_prompt(u, h0r, h0i, ssm_w, batch, t, tc):
    bwr, bwi, cwr, cwi, d_skip, tab = ssm_w
    sw = u.shape[1]
    gn = cwr.shape[0] * cwr.shape[1]
    nchunk = t // tc
    full = lambda a: pl.BlockSpec(a.shape, lambda b, c: (0,) * a.ndim)
    st_spec = pl.BlockSpec((1, 1, gn), lambda b, c: (b, 0, 0))
    u_spec = pl.BlockSpec((tc, sw), lambda b, c: (b * nchunk + c, 0))
    return pl.pallas_call(
        _ssm_prompt_kernel,
        grid=(batch, nchunk),
        in_specs=[u_spec, st_spec, st_spec, full(bwr), full(bwi), full(cwr), full(cwi),
                  full(d_skip), full(tab)],
        out_specs=(u_spec, st_spec, st_spec),
        out_shape=(jax.ShapeDtypeStruct((batch * t, sw), F32),
                   jax.ShapeDtypeStruct((batch, 1, gn), F32),
                   jax.ShapeDtypeStruct((batch, 1, gn), F32)),
        scratch_shapes=[pltpu.VMEM((tc, gn), F32), pltpu.VMEM((tc, gn), F32),
                        pltpu.VMEM((1, gn), F32), pltpu.VMEM((1, gn), F32)],
        compiler_params=_cparams("arbitrary", "arbitrary"),
        name="ssm_prompt",
    )(u, h0r, h0i, bwr, bwi, cwr, cwi, d_skip, tab)


def _ssm_sample_kernel(u_ref, h0r_ref, h0i_ref, bwr_ref, bwi_ref, cwr_ref, cwi_ref, d_ref, tab_ref,
                       y_ref, hr_ref, hi_ref, *, steps):
    sw = d_ref.shape[1]
    n_in, kw, nw = bwr_ref.shape
    ar, ai = tab_ref[6, 0:1, :], tab_ref[7, 0:1, :]
    hr, hi = h0r_ref[...], h0i_ref[...]
    for s in range(steps):
        u = u_ref[:, s * sw:(s + 1) * sw]
        ub = u.astype(BF16)
        br = jnp.concatenate([jnp.dot(ub[:, j * kw:(j + 1) * kw], bwr_ref[j], preferred_element_type=F32)
                              for j in range(n_in)], axis=1)
        bi = jnp.concatenate([jnp.dot(ub[:, j * kw:(j + 1) * kw], bwi_ref[j], preferred_element_type=F32)
                              for j in range(n_in)], axis=1)
        hr, hi = (ar * hr - ai * hi) + br, (ar * hi + ai * hr) + bi
        y_ref[:, s * sw:(s + 1) * sw] = _ssm_readout(hr, hi, cwr_ref, cwi_ref) + d_ref[...] * u
    hr_ref[...] = hr
    hi_ref[...] = hi


def _ssm_sample(u2, h0r, h0i, ssm_w, steps):
    bwr, bwi, cwr, cwi, d_skip, tab = ssm_w
    nseq = u2.shape[0]
    gn = cwr.shape[0] * cwr.shape[1]
    return pl.pallas_call(
        functools.partial(_ssm_sample_kernel, steps=steps),
        out_shape=(jax.ShapeDtypeStruct(u2.shape, F32),
                   jax.ShapeDtypeStruct((nseq, gn), F32),
                   jax.ShapeDtypeStruct((nseq, gn), F32)),
        compiler_params=pltpu.CompilerParams(vmem_limit_bytes=VMEM_LIMIT),
        name="ssm_sample",
    )(u2, h0r, h0i, bwr, bwi, cwr, cwi, d_skip, tab)


def _finish_kernel(x_ref, at_ref, ga_ref, ys_ref, gs_ref, wglu_ref, bglu_ref, woa_ref, wos_ref,
                   lng_ref, lnb_ref, o_ref, *, alpha):
    a = at_ref[...] * jax.nn.silu(ga_ref[...])
    g = jax.nn.gelu(ys_ref[...], approximate=True)
    z = jnp.dot(g.astype(BF16), wglu_ref[...], preferred_element_type=F32) + bglu_ref[...]
    s = g * jax.nn.sigmoid(z) * jax.nn.silu(gs_ref[...])
    mix = (jnp.dot(a.astype(BF16), woa_ref[...], preferred_element_type=F32)
           + jnp.dot(s.astype(BF16), wos_ref[...], preferred_element_type=F32))
    r = alpha * x_ref[...] + mix
    mean = jnp.mean(r, axis=-1, keepdims=True)
    cen = r - mean
    var = jnp.mean(cen * cen, axis=-1, keepdims=True)
    o_ref[...] = cen * lax.rsqrt(var + LN_EPS) * lng_ref[...] + lnb_ref[...]


def _finish(x2, attn, g_att, y_ssm, g_ssm, fin_w, tm, alpha):
    rows, d = x2.shape
    row_spec = lambda a: pl.BlockSpec((tm, a.shape[1]), lambda i: (i, 0))
    full = lambda a: pl.BlockSpec(a.shape, lambda i: (0,) * a.ndim)
    acts = (x2, attn, g_att, y_ssm, g_ssm)
    return pl.pallas_call(
        functools.partial(_finish_kernel, alpha=alpha),
        grid=(rows // tm,),
        in_specs=[row_spec(a) for a in acts] + [full(w) for w in fin_w],
        out_specs=pl.BlockSpec((tm, d), lambda i: (i, 0)),
        out_shape=jax.ShapeDtypeStruct((rows, d), F32),
        compiler_params=_cparams("arbitrary"),
        name="finish",
    )(*acts, *fin_w)


def _page_copy(pt_ref, src_hbm, buf, sem, seq, first_page, slot, buf_off, j):
    return pltpu.make_async_copy(src_hbm.at[pt_ref[seq, first_page + j]], buf.at[slot, buf_off + j],
                                 sem.at[slot])


def _start_pages(n, *copy_args):
    def go(j, carry):
        _page_copy(*copy_args, j).start()
        return carry
    lax.fori_loop(0, n, go, 0)


def _wait_pages(n, *copy_args):
    def go(j, carry):
        _page_copy(*copy_args, j).wait()
        return carry
    lax.fori_loop(0, n, go, 0)


def _sample_index_kernel(pt_ref, qi_ref, wi_ref, kin_ref, kidx_hbm, bias_ref, buf, sem,
                         *, n_pages, page, steps, k_sel, group):
    s = pl.program_id(0)
    nsteps = pl.num_programs(0)
    slot = s % 2

    def for_group(fn, step, sl):
        for g in range(group):
            fn(n_pages, pt_ref, kidx_hbm, buf, sem, step * group + g, 0, sl, g * n_pages)

    @pl.when(s == 0)
    def _():
        for_group(_start_pages, 0, 0)

    @pl.when(s + 1 < nsteps)
    def _():
        for_group(_start_pages, s + 1, 1 - slot)

    for_group(_wait_pages, s, slot)

    past = n_pages * page
    nt = (((1,), (1,)), ((), ()))
    w_scale = (IDX_DIM ** -0.5) * (N_IDX_HEADS ** -0.5)
    pad = jnp.zeros((LANES - SUBLANES, IDX_DIM), F32)

    def head_sum(d, wi):
        w = jnp.maximum(d, 0.0) * wi
        out = w[0:SUBLANES]
        for h in range(1, N_IDX_HEADS):
            out = out + w[h * SUBLANES:(h + 1) * SUBLANES]
        return out

    scores = []
    for g in range(group):
        qi = qi_ref[g]
        wi = wi_ref[g] * w_scale
        ki_t = jnp.concatenate([buf[slot, g * n_pages + j].astype(BF16) for j in range(n_pages)],
                               axis=1)
        ki_new = jnp.concatenate([kin_ref[g][:, :IDX_DIM], pad], axis=0).astype(BF16)
        sc_past = head_sum(jnp.dot(qi, ki_t, preferred_element_type=F32), wi)
        sc_new = head_sum(lax.dot_general(qi, ki_new, nt, preferred_element_type=F32), wi)
        scores.append(jnp.concatenate([sc_past, sc_new], axis=1))
    score = jnp.concatenate(scores, axis=0)
    shape = score.shape
    kpos = lax.broadcasted_iota(I32, shape, 1)
    qstep = lax.broadcasted_iota(I32, shape, 0) & (SUBLANES - 1)
    valid = kpos <= jnp.minimum(past + qstep, past + steps - 1)
    score = jnp.where(valid, score, NEG_INF)
    score = jnp.where(qstep < steps, score, -kpos.astype(F32))
    bias = jnp.where(valid, _topk_select(score, kpos, k_sel, axis=1), NEG_INF)
    for g in range(group):
        bias_ref[g] = bias[g * SUBLANES:(g + 1) * SUBLANES]


def _sample_index(page_table, qi_hs, wi_hs, kw_new, kidx_pages, steps, k_sel, group):
    nseq, n_pages = page_table.shape
    page = kidx_pages.shape[2]
    past = n_pages * page
    blk = lambda a: pl.BlockSpec((group,) + a.shape[1:], lambda s, pt: (s, 0, 0))
    grid_spec = pltpu.PrefetchScalarGridSpec(
        num_scalar_prefetch=1,
        grid=(nseq // group,),
        in_specs=[blk(qi_hs), blk(wi_hs), blk(kw_new), pl.BlockSpec(memory_space=pl.ANY)],
        out_specs=pl.BlockSpec((group, SUBLANES, past + LANES), lambda s, pt: (s, 0, 0)),
        scratch_shapes=[pltpu.VMEM((2, group * n_pages, IDX_DIM, page), F32),
                        pltpu.SemaphoreType.DMA((2,))],
    )
    return pl.pallas_call(
        functools.partial(_sample_index_kernel, n_pages=n_pages, page=page, steps=steps, k_sel=k_sel,
                          group=group),
        grid_spec=grid_spec,
        out_shape=jax.ShapeDtypeStruct((nseq, SUBLANES, past + LANES), F32),
        compiler_params=_cparams("arbitrary"),
        name="sample_index",
    )(page_table, qi_hs, wi_hs, kw_new, kidx_pages)


def _sample_attn_kernel(pt_ref, q_ref, bias_ref, biasn_ref, kn_ref, vn_ref, k_hbm, v_hbm, o_ref,
                        kbuf, vbuf, ksem, vsem, m_scr, l_scr, acc_scr,
                        *, n_chunks, cpages, page, n_heads):
    step = pl.program_id(0)
    nsteps = pl.num_programs(0)
    slot = step % 2

    def both(fn, item, sl):
        seq, ch = item // n_chunks, item % n_chunks
        fn(cpages, pt_ref, k_hbm, kbuf, ksem, seq, ch * cpages, sl, 0)
        fn(cpages, pt_ref, v_hbm, vbuf, vsem, seq, ch * cpages, sl, 0)

    @pl.when(step == 0)
    def _():
        both(_start_pages, step, 0)

    @pl.when(step + 1 < nsteps)
    def _():
        both(_start_pages, step + 1, 1 - slot)

    both(_wait_pages, step, slot)
    ch = step % n_chunks

    @pl.when(ch == 0)
    def _():
        m_scr[...] = jnp.full(m_scr.shape, NEG_INF, F32)
        l_scr[...] = jnp.zeros(l_scr.shape, F32)
        acc_scr[...] = jnp.zeros(acc_scr.shape, F32)

    nt = (((1,), (1,)), ((), ()))

    def update(k_t, v_t, bias8):
        sc = jnp.concatenate([jnp.dot(q_ref[0, h], k_t[h], preferred_element_type=F32)
                              for h in range(n_heads)], axis=0)
        sc = sc + jnp.concatenate([bias8] * n_heads, axis=0)
        m_old = m_scr[...]
        m_new = jnp.maximum(m_old, jnp.max(sc, axis=1, keepdims=True))
        m_safe = jnp.where(m_new == NEG_INF, 0.0, m_new)
        alpha = jnp.exp(m_old - m_safe)
        p = jnp.exp(sc - m_safe)
        l_scr[...] = alpha * l_scr[...] + jnp.sum(p, axis=1, keepdims=True)
        pv = jnp.concatenate([lax.dot_general(p[h * SUBLANES:(h + 1) * SUBLANES].astype(BF16), v_t[h], nt,
                                              preferred_element_type=F32)
                              for h in range(n_heads)], axis=0)
        acc_scr[...] = alpha * acc_scr[...] + pv
        m_scr[...] = m_new

    def chunk_t(buf, h):
        return jnp.concatenate([buf[slot, j, h].astype(BF16) for j in range(cpages)], axis=1)

    update([chunk_t(kbuf, h) for h in range(n_heads)], [chunk_t(vbuf, h) for h in range(n_heads)],
           bias_ref[0])

    @pl.when(ch == n_chunks - 1)
    def _():
        pad = jnp.zeros((LANES - SUBLANES, n_heads * HEAD_DIM), F32)

        def new_t(ref):
            x_t = jnp.concatenate([ref[0], pad], axis=0).T
            return [x_t[h * HEAD_DIM:(h + 1) * HEAD_DIM].astype(BF16) for h in range(n_heads)]

        update(new_t(kn_ref), new_t(vn_ref), biasn_ref[0])
        o = acc_scr[...] / l_scr[...]
        o_ref[0] = jnp.concatenate([o[h * SUBLANES:(h + 1) * SUBLANES] for h in range(n_heads)], axis=1)


def _sample_attention(page_table, q_hs, bias, k_new, v_new, k_pages, v_pages, cpages):
    nseq, n_pages = page_table.shape
    n_heads, page = k_pages.shape[1], k_pages.shape[3]
    aw = n_heads * HEAD_DIM
    n_chunks = n_pages // cpages
    cs = cpages * page
    rows = n_heads * SUBLANES
    grid_spec = pltpu.PrefetchScalarGridSpec(
        num_scalar_prefetch=1,
        grid=(nseq * n_chunks,),
        in_specs=[pl.BlockSpec((1, n_heads, SUBLANES, HEAD_DIM), lambda i, pt: (i // n_chunks, 0, 0, 0)),
                  pl.BlockSpec((1, SUBLANES, cs), lambda i, pt: (i // n_chunks, 0, i % n_chunks)),
                  pl.BlockSpec((1, SUBLANES, LANES), lambda i, pt: (i // n_chunks, 0, n_pages * page // LANES)),
                  pl.BlockSpec((1, SUBLANES, aw), lambda i, pt: (i // n_chunks, 0, 0)),
                  pl.BlockSpec((1, SUBLANES, aw), lambda i, pt: (i // n_chunks, 0, 0)),
                  pl.BlockSpec(memory_space=pl.ANY),
                  pl.BlockSpec(memory_space=pl.ANY)],
        out_specs=pl.BlockSpec((1, SUBLANES, aw), lambda i, pt: (i // n_chunks, 0, 0)),
        scratch_shapes=[pltpu.VMEM((2, cpages, n_heads, HEAD_DIM, page), F32),
                        pltpu.VMEM((2, cpages, n_heads, HEAD_DIM, page), F32),
                        pltpu.SemaphoreType.DMA((2,)),
                        pltpu.SemaphoreType.DMA((2,)),
                        pltpu.VMEM((rows, 1), F32),
                        pltpu.VMEM((rows, 1), F32),
                        pltpu.VMEM((rows, HEAD_DIM), F32)],
    )
    return pl.pallas_call(
        functools.partial(_sample_attn_kernel, n_chunks=n_chunks, cpages=cpages, page=page,
                          n_heads=n_heads),
        grid_spec=grid_spec,
        out_shape=jax.ShapeDtypeStruct((nseq, SUBLANES, aw), F32),
        compiler_params=_cparams("arbitrary"),
        name="sample_attn",
    )(page_table, q_hs, bias, bias, k_new, v_new, k_pages, v_pages)


def _pack_w_in(w, aw, sw):
    sizes = [aw] * 4 + [N_IDX_HEADS * IDX_DIM, IDX_DIM, N_IDX_HEADS, sw, sw]
    offs = np.cumsum([0] + sizes)
    seg = [w[:, offs[j]:offs[j + 1]] for j in range(len(sizes))]
    pad = jnp.zeros((w.shape[0], LANES - IDX_DIM - N_IDX_HEADS), w.dtype)
    return jnp.concatenate(seg[:5] + [seg[5], seg[6], pad] + seg[7:], axis=1).astype(BF16)


def _ssm_weights(a_re, a_im, log_dt, b_re, b_im, c_re, c_im, d_skip):
    g, n, c = b_re.shape
    a_re, a_im = a_re.astype(F32), a_im.astype(F32)
    dt = jnp.exp(log_dt.astype(F32))[:, None]
    mag = jnp.exp(dt * a_re)
    abr = mag * jnp.cos(dt * a_im)
    abi = mag * jnp.sin(dt * a_im)
    den = a_re * a_re + a_im * a_im
    nr = abr - 1.0
    cr = (nr * a_re + abi * a_im) / den
    ci = (abi * a_re - nr * a_im) / den
    b_re, b_im = b_re.astype(F32), b_im.astype(F32)
    bbr = cr[..., None] * b_re - ci[..., None] * b_im
    bbi = cr[..., None] * b_im + ci[..., None] * b_re

    kw_lanes = 2 * LANES
    gpk = kw_lanes // c
    n_in = g // gpk
    eye = jnp.eye(gpk, dtype=F32)

    def in_map(bb):
        bb = bb.reshape(n_in, gpk, n, c)
        return jnp.einsum("jgnc,gh->jgchn", bb, eye).reshape(n_in, gpk * c, gpk * n).astype(BF16)

    gps = LANES // c

    def out_map(cc):
        cc = cc.astype(F32).reshape(g // gps, gps, c, n)
        return jnp.einsum("sgcn,gh->sgnhc", cc, jnp.eye(gps, dtype=F32)).reshape(
            g // gps, gps * n, gps * c).astype(BF16)

    ar, ai = abr.reshape(1, g * n), abi.reshape(1, g * n)
    pows_r, pows_i = [ar], [ai]
    for _ in range(SUBLANES - 1):
        pr, pi = pows_r[-1], pows_i[-1]
        pows_r.append(pr * ar - pi * ai)
        pows_i.append(pr * ai + pi * ar)
    row = jnp.arange(SUBLANES)[:, None]

    def masked(k):
        keep = row >= k
        return (jnp.where(keep, pows_r[k - 1], 0.0), jnp.where(keep, pows_i[k - 1], 0.0))

    m1, m2, m4 = masked(1), masked(2), masked(4)
    tab = jnp.stack([m1[0], m1[1], m2[0], m2[1], m4[0], m4[1],
                     jnp.concatenate(pows_r, axis=0), jnp.concatenate(pows_i, axis=0)], axis=0)
    return (in_map(bbr), in_map(bbi), out_map(c_re), out_map(c_im),
            d_skip.astype(F32).reshape(1, g * c), tab)


def kernel(x_prompt, x_sample, cache_k, cache_v, cache_kidx, page_table, state_ssm_re, state_ssm_im,
           w_in, w_out, ssm_a_re, ssm_a_im, ssm_log_dt, ssm_b_re, ssm_b_im, ssm_c_re, ssm_c_im,
           ssm_d, w_glu, b_glu, ln_g, ln_b):
    depth = w_in.shape[0]
    assert depth == 1, "single-layer trunk"
    batch, t, d = x_prompt.shape
    nseq, steps, _ = x_sample.shape
    n_pool, page, n_heads, head_dim = cache_k.shape[1:]
    assert head_dim == HEAD_DIM and steps <= SUBLANES
    aw = n_heads * HEAD_DIM
    g_ssm_n = ssm_b_re.shape[1]
    sw = g_ssm_n * SSM_GROUP
    gn = g_ssm_n * SSM_STATE
    n_pages = page_table.shape[1]
    past = n_pages * page
    alpha = (2.0 * depth) ** 0.25

    layer = 0
    w_packed = _pack_w_in(w_in[layer], aw, sw)
    ssm_w = _ssm_weights(ssm_a_re[layer], ssm_a_im[layer], ssm_log_dt[layer], ssm_b_re[layer],
                         ssm_b_im[layer], ssm_c_re[layer], ssm_c_im[layer], ssm_d[layer])
    fin_w = (w_glu[layer].astype(BF16), b_glu[layer].astype(F32).reshape(1, sw),
             w_out[layer][:aw].astype(BF16), w_out[layer][aw:].astype(BF16),
             ln_g[layer].astype(F32).reshape(1, d), ln_b[layer].astype(F32).reshape(1, d))

    tm = 512
    xp2 = x_prompt.reshape(batch * t, d)
    q, k_t, v_t, g_att, qi, kw, ki_t, u, g_ssm = _project(
        xp2, w_packed, _rope_tables(jnp.arange(t)), tm, batch, aw, sw)
    attn = _prompt_attention(q, qi, kw, k_t, v_t, batch, t, n_heads, min(TOPK_MAX, t // 4))
    zeros_state = jnp.zeros((batch, 1, gn), F32)
    y_ssm, hr_p, hi_p = _ssm_prompt(u, zeros_state, zeros_state, ssm_w, batch, t, min(512, t))
    y_prompt = _finish(xp2, attn, g_att, y_ssm, g_ssm, fin_w, tm, alpha).reshape(batch, t, d)
    to_cache = lambda a_t: jnp.transpose(a_t.reshape(1, batch, n_heads, HEAD_DIM, t), (0, 1, 4, 2, 3))
    new_k_p = to_cache(k_t)
    new_v_p = to_cache(v_t)
    new_ki_p = jnp.transpose(ki_t, (0, 2, 1))[None]
    new_hr_p = hr_p.reshape(1, batch, g_ssm_n, SSM_STATE)
    new_hi_p = hi_p.reshape(1, batch, g_ssm_n, SSM_STATE)

    rows_s = nseq * steps
    xs2 = x_sample.reshape(rows_s, d)
    pos_s = past + (jnp.arange(rows_s) % steps)
    qs, ks_t, vs_t, gas, qis, kws, _, us, gss = _project(
        xs2, w_packed, _rope_tables(pos_s), rows_s, 1, aw, sw)
    ks, vs = ks_t[0].T, vs_t[0].T

    def head_rows(a, width):
        a = a.reshape(nseq, steps, -1, width).transpose(0, 2, 1, 3)
        return jnp.pad(a, ((0, 0), (0, 0), (0, SUBLANES - steps), (0, 0)))

    def step_rows(a):
        return jnp.pad(a.reshape(nseq, steps, -1), ((0, 0), (0, SUBLANES - steps), (0, 0)))

    qi_hs = head_rows(qis.T, IDX_DIM).reshape(nseq, N_IDX_HEADS * SUBLANES, IDX_DIM)
    wi_hs = head_rows(kws[:, IDX_DIM:IDX_DIM + N_IDX_HEADS], 1).reshape(nseq, N_IDX_HEADS * SUBLANES, 1)
    q_hs = head_rows(qs.T, HEAD_DIM)

    kidx_t = jnp.transpose(cache_kidx[layer], (0, 2, 1))
    ck_t = jnp.transpose(cache_k[layer], (0, 2, 3, 1))
    cv_t = jnp.transpose(cache_v[layer], (0, 2, 3, 1))

    k_sel_s = min(TOPK_MAX, (past + steps) // 4)
    group = 4 if nseq % 4 == 0 else 1
    bias = _sample_index(page_table, qi_hs, wi_hs, step_rows(kws), kidx_t, steps, k_sel_s, group)
    cpages = 16 if n_pages % 16 == 0 else n_pages
    attn_s = _sample_attention(page_table, q_hs, bias, step_rows(ks), step_rows(vs), ck_t, cv_t, cpages)
    attn_s = attn_s[:, :steps].reshape(rows_s, aw)

    y_s, hr_s, hi_s = _ssm_sample(us.reshape(nseq, steps * sw),
                                  state_ssm_re[layer].reshape(nseq, gn).astype(F32),
                                  state_ssm_im[layer].reshape(nseq, gn).astype(F32), ssm_w, steps)
    y_sample = _finish(xs2, attn_s, gas, y_s.reshape(rows_s, sw), gss, fin_w, rows_s, alpha)
    y_sample = y_sample.reshape(nseq, steps, d)

    return (y_prompt, y_sample, new_k_p, new_v_p, new_ki_p, new_hr_p, new_hi_p,
            ks.reshape(1, nseq, steps, n_heads, HEAD_DIM), vs.reshape(1, nseq, steps, n_heads, HEAD_DIM),
            kws[:, :IDX_DIM].reshape(1, nseq, steps, IDX_DIM),
            hr_s.reshape(1, nseq, g_ssm_n, SSM_STATE), hi_s.reshape(1, nseq, g_ssm_n, SSM_STATE))
```

```python
import functools
import math

import jax
import jax.numpy as jnp
import numpy as np
from jax import lax
from jax.experimental import pallas as pl
from jax.experimental.pallas import tpu as pltpu

F32 = jnp.float32
BF16 = jnp.bfloat16
I32 = jnp.int32

HEAD_DIM = 64
ROT_DIM = HEAD_DIM // 4
ROPE_THETA = 500000.0
N_IDX_HEADS = 8
IDX_DIM = 64
TOPK_MAX = 256
Q_BLOCK = 128
SSM_GROUP = 16
SSM_STATE = 64
LN_EPS = 1e-5

LANES = 128
SUBLANES = 8
VMEM_LIMIT = 52 * 1024 * 1024

INT_MIN = -(2 ** 31)
NEG_INF = float("-inf")


def _cparams(*sem):
    return pltpu.CompilerParams(dimension_semantics=sem, vmem_limit_bytes=VMEM_LIMIT)


def _ordered_to_float(o):
    return pltpu.bitcast(jnp.where(o < 0, o ^ jnp.int32(0x7FFFFFFF), o), F32)


REDUCE_CHAINS = 8


def _reduce(x, kind, axis):
    pair, full = (jnp.add, jnp.sum) if kind == "sum" else (jnp.maximum, jnp.max)
    n = x.shape[axis]
    unit = SUBLANES if axis == 0 else LANES
    if n % unit == 0 and n // unit > REDUCE_CHAINS:
        nv = n // unit
        chains = min((d for d in range(1, nv + 1) if nv % d == 0), key=lambda d: abs(d - REDUCE_CHAINS))
        width = unit * chains
        take = (lambda c: x[c:c + width, :]) if axis == 0 else (lambda c: x[:, c:c + width])
        acc = take(0)
        for c in range(width, n, width):
            acc = pair(acc, take(c))
        x = acc
    return full(x, axis=axis, keepdims=True)


def _topk_select(score, kpos, k_sel, axis):
    kf = float(k_sel)
    n_idx_bits = max(1, int(math.ceil(math.log2(score.shape[axis] + 1))))
    qshape = tuple(1 if a == axis else n for a, n in enumerate(score.shape))

    def count(mask):
        return _reduce(jnp.where(mask, 1.0, 0.0), "sum", axis)

    ans0 = jnp.where(count(score >= 0.0) >= kf, jnp.int32(0), jnp.int32(INT_MIN))

    def val_step(b, ans):
        cand = ans + jnp.left_shift(jnp.int32(1), jnp.int32(30) - b)
        return jnp.where(count(score >= _ordered_to_float(cand)) >= kf, cand, ans)

    enough = count(score > NEG_INF) >= kf
    thr = jnp.where(enough, _ordered_to_float(lax.fori_loop(0, 31, val_step, ans0)), NEG_INF)
    gt = score > thr
    eq = score == thr
    need = kf - count(gt)
    has_ties = jnp.max(jnp.where(enough, count(eq) - need, 0.0)) > 0.0

    def tie_search():
        def idx_step(b, j):
            cand = j + jnp.left_shift(jnp.int32(1), jnp.int32(n_idx_bits - 1) - b)
            c = count(jnp.where(eq, kpos, jnp.int32(2 ** 30)) < cand)
            return jnp.where(c < need, cand, j)
        return lax.fori_loop(0, n_idx_bits, idx_step, jnp.zeros(qshape, I32))

    def no_ties():
        return jnp.full(qshape, 2 ** 30, I32)

    jthr = lax.cond(has_ties, tie_search, no_ties)
    tie_bias = jnp.where(kpos <= jthr, 0.0, NEG_INF)
    return jnp.where(gt, 0.0, jnp.where(eq, tie_bias, NEG_INF))


def _rope(x, cosf, sin_up, sin_dn):
    half = ROT_DIM // 2
    return (x * cosf + pltpu.roll(x, LANES - half, axis=1) * sin_up
            + pltpu.roll(x, half, axis=1) * sin_dn)


def _proj_kernel(x_ref, w_ref, cos_ref, sup_ref, sdn_ref,
                 q_ref, k_ref, v_ref, ga_ref, qi_ref, kw_ref, ki_ref, u_ref, gs_ref, v_scr, *, aw, sw):
    xb = x_ref[...].astype(BF16)
    cosf, sup, sdn = cos_ref[...], sup_ref[...], sdn_ref[...]

    def seg(off, width):
        return jnp.dot(xb, w_ref[:, off:off + width], preferred_element_type=F32)

    def rope_seg(h):
        return jnp.concatenate(
            [_rope(h[:, c:c + LANES], cosf, sup, sdn) for c in range(0, h.shape[1], LANES)], axis=1)

    off = 0
    q_ref[...] = (rope_seg(seg(off, aw)) * (HEAD_DIM ** -0.5)).T.astype(BF16); off += aw
    k_ref[0] = rope_seg(seg(off, aw)).T; off += aw
    v_scr[...] = seg(off, aw); off += aw
    v_ref[0] = v_scr[...].T
    ga_ref[...] = seg(off, aw); off += aw
    qi_ref[...] = rope_seg(seg(off, N_IDX_HEADS * IDX_DIM)).T.astype(BF16); off += N_IDX_HEADS * IDX_DIM
    kw = seg(off, LANES); off += LANES
    lane = lax.broadcasted_iota(I32, kw.shape, 1)
    is_key = lane < IDX_DIM
    kw = _rope(kw, jnp.where(is_key, cosf, 1.0), jnp.where(is_key, sup, 0.0),
               jnp.where(is_key, sdn, 0.0))
    kw_ref[...] = kw
    ki_ref[0] = kw.T[:IDX_DIM, :]
    u_ref[...] = seg(off, sw); off += sw
    gs_ref[...] = seg(off, sw)


def _rope_tables(pos):
    half = ROT_DIM // 2
    inv = ROPE_THETA ** (-2.0 * jnp.arange(half, dtype=F32) / ROT_DIM)
    ang = pos.astype(F32)[:, None] * inv[None, :]
    cos, sin = jnp.cos(ang), jnp.sin(ang)
    p = pos.shape[0]
    ones = jnp.ones((p, HEAD_DIM - ROT_DIM), F32)
    zeros = jnp.zeros((p, HEAD_DIM - ROT_DIM), F32)
    zh = jnp.zeros((p, half), F32)
    cos64 = jnp.concatenate([cos, cos, ones], axis=1)
    up64 = jnp.concatenate([-sin, zh, zeros], axis=1)
    dn64 = jnp.concatenate([zh, sin, zeros], axis=1)
    rep = LANES // HEAD_DIM
    return (jnp.tile(cos64, (1, rep)), jnp.tile(up64, (1, rep)), jnp.tile(dn64, (1, rep)))


def _project(x2, w_packed, tables, tm, nseq, aw, sw):
    rows, d = x2.shape
    t = rows // nseq
    nblk = t // tm
    n_tab_blocks = tables[0].shape[0] // tm
    row_spec = lambda width: pl.BlockSpec((tm, width), lambda i: (i, 0))
    col_spec = lambda height: pl.BlockSpec((height, tm), lambda i: (0, i))
    seq_spec = lambda height: pl.BlockSpec((1, height, tm), lambda i: (i // nblk, 0, i % nblk))
    tab_spec = pl.BlockSpec((tm, LANES), lambda i: (i % n_tab_blocks, 0))
    idx_w = N_IDX_HEADS * IDX_DIM
    out_shapes = (
        jax.ShapeDtypeStruct((aw, rows), BF16),
        jax.ShapeDtypeStruct((nseq, aw, t), F32),
        jax.ShapeDtypeStruct((nseq, aw, t), F32),
        jax.ShapeDtypeStruct((rows, aw), F32),
        jax.ShapeDtypeStruct((idx_w, rows), BF16),
        jax.ShapeDtypeStruct((rows, LANES), F32),
        jax.ShapeDtypeStruct((nseq, IDX_DIM, t), F32),
        jax.ShapeDtypeStruct((rows, sw), F32),
        jax.ShapeDtypeStruct((rows, sw), F32),
    )
    out_specs = (col_spec(aw), seq_spec(aw), seq_spec(aw), row_spec(aw), col_spec(idx_w),
                 row_spec(LANES), seq_spec(IDX_DIM), row_spec(sw), row_spec(sw))
    return pl.pallas_call(
        functools.partial(_proj_kernel, aw=aw, sw=sw),
        grid=(rows // tm,),
        in_specs=[row_spec(d), pl.BlockSpec(w_packed.shape, lambda i: (0, 0)),
                  tab_spec, tab_spec, tab_spec],
        out_specs=out_specs,
        out_shape=out_shapes,
        scratch_shapes=[pltpu.VMEM((tm, aw), F32)],
        compiler_params=_cparams("arbitrary"),
        name="in_proj",
    )(x2, w_packed, *tables)


KEY_CHUNK = 2048
HEADS_PER_DOT = LANES // HEAD_DIM
IDX_HEADS_PER_DOT = 4


def _prompt_attn_kernel(qt_ref, qit_ref, kwq_ref, k_ref, v_ref, kw_ref, o_ref,
                        kb_ref, vt_ref, kib_ref, score_ref, bias_ref, *, n_heads, k_sel, n_groups):
    i = pl.program_id(1)
    t = kw_ref.shape[0]

    @pl.when(i == 0)
    def _():
        for h in range(n_heads):
            vt_ref[h] = v_ref[0, h * HEAD_DIM:(h + 1) * HEAD_DIM, :].astype(BF16)
        for c in range(0, t, LANES):
            kc_rows = k_ref[0, :, c:c + LANES].T
            for p in range(n_heads // HEADS_PER_DOT):
                kb_ref[p, c:c + LANES, :] = kc_rows[:, p * LANES:(p + 1) * LANES].astype(BF16)
        kib_ref[...] = kw_ref[:, :IDX_DIM].astype(BF16)

    w_scale = (IDX_DIM ** -0.5) * (N_IDX_HEADS ** -0.5)

    def head_pair_rhs(p):
        qa = qt_ref[(2 * p) * HEAD_DIM:(2 * p + 1) * HEAD_DIM, :]
        qb = qt_ref[(2 * p + 1) * HEAD_DIM:(2 * p + 2) * HEAD_DIM, :]
        z = jnp.zeros_like(qa)
        return jnp.concatenate([jnp.concatenate([qa, z], axis=1), jnp.concatenate([z, qb], axis=1)], axis=0)

    def body(s):
        kc = max(d for d in range(LANES, KEY_CHUNK + 1, LANES) if s % d == 0)
        w_t = kwq_ref[...].T[IDX_DIM:IDX_DIM + N_IDX_HEADS, :] * w_scale
        qpos = i * Q_BLOCK + lax.broadcasted_iota(I32, (kc, Q_BLOCK), 1)
        row = lax.broadcasted_iota(I32, (kc, Q_BLOCK), 0)
        idx_rhs = [jnp.concatenate([qit_ref[h * IDX_DIM:(h + 1) * IDX_DIM, :]
                                    for h in range(g, g + IDX_HEADS_PER_DOT)], axis=1)
                   for g in range(0, N_IDX_HEADS, IDX_HEADS_PER_DOT)]
        for c in range(0, s, kc):
            kib = kib_ref[c:c + kc, :]
            score = None
            for g, rhs in enumerate(idx_rhs):
                d = jnp.dot(kib, rhs, preferred_element_type=F32)
                for j in range(IDX_HEADS_PER_DOT):
                    h = g * IDX_HEADS_PER_DOT + j
                    term = jnp.maximum(d[:, j * Q_BLOCK:(j + 1) * Q_BLOCK], 0.0) * w_t[h:h + 1, :]
                    score = term if score is None else score + term
            score_ref[c:c + kc, :] = jnp.where(c + row <= qpos, score, NEG_INF)

        kpos = lax.broadcasted_iota(I32, (s, Q_BLOCK), 0)
        valid = kpos <= i * Q_BLOCK + lax.broadcasted_iota(I32, (s, Q_BLOCK), 1)
        bias_ref[0:s, :] = jnp.where(valid, _topk_select(score_ref[0:s, :], kpos, k_sel, axis=0), NEG_INF)

        m, l, acc = [None] * n_heads, [None] * n_heads, [None] * n_heads
        pair_rhs = [head_pair_rhs(p) for p in range(n_heads // HEADS_PER_DOT)]
        for c in range(0, s, kc):
            bias = bias_ref[c:c + kc, :]
            for h in range(n_heads):
                if h % HEADS_PER_DOT == 0:
                    sc_pair = jnp.dot(kb_ref[h // HEADS_PER_DOT, c:c + kc, :], pair_rhs[h // HEADS_PER_DOT],
                                      preferred_element_type=F32)
                j = h % HEADS_PER_DOT
                sc = sc_pair[:, j * Q_BLOCK:(j + 1) * Q_BLOCK] + bias
                m_c = _reduce(sc, "max", 0)
                m_new = m_c if c == 0 else jnp.maximum(m[h], m_c)
                m_safe = jnp.where(m_new == NEG_INF, 0.0, m_new)
                p = jnp.exp(sc - m_safe)
                l_c = _reduce(p, "sum", 0)
                pv = jnp.dot(vt_ref[h, :, c:c + kc], p.astype(BF16), preferred_element_type=F32)
                if c == 0:
                    l[h], acc[h] = l_c, pv
                else:
                    alpha = jnp.exp(m[h] - m_safe)
                    l[h], acc[h] = alpha * l[h] + l_c, alpha * acc[h] + pv
                m[h] = m_new
        o_ref[...] = jnp.concatenate([acc[h] / l[h] for h in range(n_heads)], axis=0).T

    blocks_per_group = (t // Q_BLOCK) // n_groups
    for g in range(n_groups):
        pl.when(i // blocks_per_group == g)(
            functools.partial(body, (g + 1) * blocks_per_group * Q_BLOCK))


def _prompt_attention(q_t, qi_t, kw, k_t, v_t, batch, t, n_heads, k_sel):
    aw = n_heads * HEAD_DIM
    nq = t // Q_BLOCK
    n_groups = max(g for g in (1, 2, 4, 8, 16) if nq % g == 0)
    qspec = lambda width: pl.BlockSpec((Q_BLOCK, width), lambda b, i: (b * nq + i, 0))
    tspec = lambda rows: pl.BlockSpec((rows, Q_BLOCK), lambda b, i: (0, b * nq + i))
    sspec = lambda width: pl.BlockSpec((t, width), lambda b, i: (b, 0))
    seq_t = pl.BlockSpec((1, aw, t), lambda b, i: (b, 0, 0))
    return pl.pallas_call(
        functools.partial(_prompt_attn_kernel, n_heads=n_heads, k_sel=k_sel, n_groups=n_groups),
        grid=(batch, nq),
        in_specs=[tspec(aw), tspec(N_IDX_HEADS * IDX_DIM), qspec(LANES),
                  seq_t, seq_t, sspec(LANES)],
        out_specs=qspec(aw),
        out_shape=jax.ShapeDtypeStruct((batch * t, aw), F32),
        scratch_shapes=[pltpu.VMEM((n_heads // HEADS_PER_DOT, t, LANES), BF16),
                        pltpu.VMEM((n_heads, HEAD_DIM, t), BF16),
                        pltpu.VMEM((t, IDX_DIM), BF16),
                        pltpu.VMEM((t, Q_BLOCK), F32),
                        pltpu.VMEM((t, Q_BLOCK), F32)],
        compiler_params=_cparams("arbitrary", "arbitrary"),
        name="prompt_attn",
    )(q_t, qi_t, kw, k_t, v_t, kw)


SCAN_LANES = 512


def _ssm_readout(hr, hi, cwr_ref, cwi_ref):
    n_slab, kdim, _ = cwr_ref.shape
    hr, hi = hr.astype(BF16), hi.astype(BF16)
    return jnp.concatenate(
        [jnp.dot(hr[:, s * kdim:(s + 1) * kdim], cwr_ref[s], preferred_element_type=F32)
         - jnp.dot(hi[:, s * kdim:(s + 1) * kdim], cwi_ref[s], preferred_element_type=F32)
         for s in range(n_slab)], axis=1)


def _ssm_prompt_kernel(u_ref, h0r_ref, h0i_ref, bwr_ref, bwi_ref, cwr_ref, cwi_ref, d_ref, tab_ref,
                       y_ref, hr_ref, hi_ref, br_scr, bi_scr, cr_scr, ci_scr):
    c = pl.program_id(1)
    tc = u_ref.shape[0]
    gn = br_scr.shape[1]

    @pl.when(c == 0)
    def _():
        cr_scr[...] = h0r_ref[0]
        ci_scr[...] = h0i_ref[0]

    u = u_ref[...]
    ub = u.astype(BF16)
    n_in = bwr_ref.shape[0]
    kw = bwr_ref.shape[1]
    nw = bwr_ref.shape[2]
    for j in range(n_in):
        uj = ub[:, j * kw:(j + 1) * kw]
        br_scr[:, j * nw:(j + 1) * nw] = jnp.dot(uj, bwr_ref[j], preferred_element_type=F32)
        bi_scr[:, j * nw:(j + 1) * nw] = jnp.dot(uj, bwi_ref[j], preferred_element_type=F32)

    for lo in range(0, gn, SCAN_LANES):
        sl = slice(lo, lo + SCAN_LANES)
        a1r, a1i = tab_ref[0, :, sl], tab_ref[1, :, sl]
        a2r, a2i = tab_ref[2, :, sl], tab_ref[3, :, sl]
        a4r, a4i = tab_ref[4, :, sl], tab_ref[5, :, sl]
        pr, pi = tab_ref[6, :, sl], tab_ref[7, :, sl]

        def tile(j, carry):
            cr, ci = carry
            rows = pl.ds(pl.multiple_of(j * SUBLANES, SUBLANES), SUBLANES)
            br, bi = br_scr[rows, sl], bi_scr[rows, sl]
            for shift, ar, ai in ((1, a1r, a1i), (2, a2r, a2i), (4, a4r, a4i)):
                sr = pltpu.roll(br, shift, axis=0)
                si = pltpu.roll(bi, shift, axis=0)
                br, bi = br + (ar * sr - ai * si), bi + (ar * si + ai * sr)
            hr = br + (pr * cr - pi * ci)
            hi = bi + (pr * ci + pi * cr)
            br_scr[rows, sl] = hr
            bi_scr[rows, sl] = hi
            return hr[SUBLANES - 1:SUBLANES, :], hi[SUBLANES - 1:SUBLANES, :]

        cr, ci = lax.fori_loop(0, tc // SUBLANES, tile, (cr_scr[:, sl], ci_scr[:, sl]))
        cr_scr[:, sl] = cr
        ci_scr[:, sl] = ci

    y_ref[...] = _ssm_readout(br_scr[...], bi_scr[...], cwr_ref, cwi_ref) + d_ref[...] * u
    hr_ref[0] = cr_scr[...]
    hi_ref[0] = ci_scr[...]


def _ssm_prompt(u, h0r, h0i, ssm_w, batch, t, tc):
    bwr, bwi, cwr, cwi, d_skip, tab = ssm_w
    sw = u.shape[1]
    gn = cwr.shape[0] * cwr.shape[1]
    nchunk = t // tc
    full = lambda a: pl.BlockSpec(a.shape, lambda b, c: (0,) * a.ndim)
    st_spec = pl.BlockSpec((1, 1, gn), lambda b, c: (b, 0, 0))
    u_spec = pl.BlockSpec((tc, sw), lambda b, c: (b * nchunk + c, 0))
    return pl.pallas_call(
        _ssm_prompt_kernel,
        grid=(batch, nchunk),
        in_specs=[u_spec, st_spec, st_spec, full(bwr), full(bwi), full(cwr), full(cwi),
                  full(d_skip), full(tab)],
        out_specs=(u_spec, st_spec, st_spec),
        out_shape=(jax.ShapeDtypeStruct((batch * t, sw), F32),
                   jax.ShapeDtypeStruct((batch, 1, gn), F32),
                   jax.ShapeDtypeStruct((batch, 1, gn), F32)),
        scratch_shapes=[pltpu.VMEM((tc, gn), F32), pltpu.VMEM((tc, gn), F32),
                        pltpu.VMEM((1, gn), F32), pltpu.VMEM((1, gn), F32)],
        compiler_params=_cparams("arbitrary", "arbitrary"),
        name="ssm_prompt",
    )(u, h0r, h0i, bwr, bwi, cwr, cwi, d_skip, tab)


def _ssm_sample_kernel(u_ref, h0r_ref, h0i_ref, bwr_ref, bwi_ref, cwr_ref, cwi_ref, d_ref, tab_ref,
                       y_ref, hr_ref, hi_ref, *, steps):
    sw = d_ref.shape[1]
    n_in, kw, nw = bwr_ref.shape
    ar, ai = tab_ref[6, 0:1, :], tab_ref[7, 0:1, :]
    hr, hi = h0r_ref[...], h0i_ref[...]
    for s in range(steps):
        u = u_ref[:, s * sw:(s + 1) * sw]
        ub = u.astype(BF16)
        br = jnp.concatenate([jnp.dot(ub[:, j * kw:(j + 1) * kw], bwr_ref[j], preferred_element_type=F32)
                              for j in range(n_in)], axis=1)
        bi = jnp.concatenate([jnp.dot(ub[:, j * kw:(j + 1) * kw], bwi_ref[j], preferred_element_type=F32)
                              for j in range(n_in)], axis=1)
        hr, hi = (ar * hr - ai * hi) + br, (ar * hi + ai * hr) + bi
        y_ref[:, s * sw:(s + 1) * sw] = _ssm_readout(hr, hi, cwr_ref, cwi_ref) + d_ref[...] * u
    hr_ref[...] = hr
    hi_ref[...] = hi


def _ssm_sample(u2, h0r, h0i, ssm_w, steps):
    bwr, bwi, cwr, cwi, d_skip, tab = ssm_w
    nseq = u2.shape[0]
    gn = cwr.shape[0] * cwr.shape[1]
    return pl.pallas_call(
        functools.partial(_ssm_sample_kernel, steps=steps),
        out_shape=(jax.ShapeDtypeStruct(u2.shape, F32),
                   jax.ShapeDtypeStruct((nseq, gn), F32),
                   jax.ShapeDtypeStruct((nseq, gn), F32)),
        compiler_params=pltpu.CompilerParams(vmem_limit_bytes=VMEM_LIMIT),
        name="ssm_sample",
    )(u2, h0r, h0i, bwr, bwi, cwr, cwi, d_skip, tab)


def _finish_kernel(x_ref, at_ref, ga_ref, ys_ref, gs_ref, wglu_ref, bglu_ref, woa_ref, wos_ref,
                   lng_ref, lnb_ref, o_ref, *, alpha):
    a = at_ref[...] * jax.nn.silu(ga_ref[...])
    g = jax.nn.gelu(ys_ref[...], approximate=True)
    z = jnp.dot(g.astype(BF16), wglu_ref[...], preferred_element_type=F32) + bglu_ref[...]
    s = g * jax.nn.sigmoid(z) * jax.nn.silu(gs_ref[...])
    mix = (jnp.dot(a.astype(BF16), woa_ref[...], preferred_element_type=F32)
           + jnp.dot(s.astype(BF16), wos_ref[...], preferred_element_type=F32))
    r = alpha * x_ref[...] + mix
    mean = jnp.mean(r, axis=-1, keepdims=True)
    cen = r - mean
    var = jnp.mean(cen * cen, axis=-1, keepdims=True)
    o_ref[...] = cen * lax.rsqrt(var + LN_EPS) * lng_ref[...] + lnb_ref[...]


def _finish(x2, attn, g_att, y_ssm, g_ssm, fin_w, tm, alpha):
    rows, d = x2.shape
    row_spec = lambda a: pl.BlockSpec((tm, a.shape[1]), lambda i: (i, 0))
    full = lambda a: pl.BlockSpec(a.shape, lambda i: (0,) * a.ndim)
    acts = (x2, attn, g_att, y_ssm, g_ssm)
    return pl.pallas_call(
        functools.partial(_finish_kernel, alpha=alpha),
        grid=(rows // tm,),
        in_specs=[row_spec(a) for a in acts] + [full(w) for w in fin_w],
        out_specs=pl.BlockSpec((tm, d), lambda i: (i, 0)),
        out_shape=jax.ShapeDtypeStruct((rows, d), F32),
        compiler_params=_cparams("arbitrary"),
        name="finish",
    )(*acts, *fin_w)


def _page_copy(pt_ref, src_hbm, buf, sem, seq, first_page, slot, buf_off, j):
    return pltpu.make_async_copy(src_hbm.at[pt_ref[seq, first_page + j]], buf.at[slot, buf_off + j],
                                 sem.at[slot])


def _start_pages(n, *copy_args):
    def go(j, carry):
        _page_copy(*copy_args, j).start()
        return carry
    lax.fori_loop(0, n, go, 0)


def _wait_pages(n, *copy_args):
    def go(j, carry):
        _page_copy(*copy_args, j).wait()
        return carry
    lax.fori_loop(0, n, go, 0)


def _sample_index_kernel(pt_ref, qi_ref, wi_ref, kin_ref, kidx_hbm, bias_ref, buf, sem,
                         *, n_pages, page, steps, k_sel, group):
    s = pl.program_id(0)
    nsteps = pl.num_programs(0)
    slot = s % 2

    def for_group(fn, step, sl):
        for g in range(group):
            fn(n_pages, pt_ref, kidx_hbm, buf, sem, step * group + g, 0, sl, g * n_pages)

    @pl.when(s == 0)
    def _():
        for_group(_start_pages, 0, 0)

    @pl.when(s + 1 < nsteps)
    def _():
        for_group(_start_pages, s + 1, 1 - slot)

    for_group(_wait_pages, s, slot)

    past = n_pages * page
    nt = (((1,), (1,)), ((), ()))
    w_scale = (IDX_DIM ** -0.5) * (N_IDX_HEADS ** -0.5)
    pad = jnp.zeros((LANES - SUBLANES, IDX_DIM), F32)

    def head_sum(d, wi):
        w = jnp.maximum(d, 0.0) * wi
        out = w[0:SUBLANES]
        for h in range(1, N_IDX_HEADS):
            out = out + w[h * SUBLANES:(h + 1) * SUBLANES]
        return out

    scores = []
    for g in range(group):
        qi = qi_ref[g]
        wi = wi_ref[g] * w_scale
        ki_t = jnp.concatenate([buf[slot, g * n_pages + j].astype(BF16) for j in range(n_pages)],
                               axis=1)
        ki_new = jnp.concatenate([kin_ref[g][:, :IDX_DIM], pad], axis=0).astype(BF16)
        sc_past = head_sum(jnp.dot(qi, ki_t, preferred_element_type=F32), wi)
        sc_new = head_sum(lax.dot_general(qi, ki_new, nt, preferred_element_type=F32), wi)
        scores.append(jnp.concatenate([sc_past, sc_new], axis=1))
    score = jnp.concatenate(scores, axis=0)
    shape = score.shape
    kpos = lax.broadcasted_iota(I32, shape, 1)
    qstep = lax.broadcasted_iota(I32, shape, 0) & (SUBLANES - 1)
    valid = kpos <= jnp.minimum(past + qstep, past + steps - 1)
    score = jnp.where(valid, score, NEG_INF)
    score = jnp.where(qstep < steps, score, -kpos.astype(F32))
    bias = jnp.where(valid, _topk_select(score, kpos, k_sel, axis=1), NEG_INF)
    for g in range(group):
        bias_ref[g] = bias[g * SUBLANES:(g + 1) * SUBLANES]


def _sample_index(page_table, qi_hs, wi_hs, kw_new, kidx_pages, steps, k_sel, group):
    nseq, n_pages = page_table.shape
    page = kidx_pages.shape[2]
    past = n_pages * page
    blk = lambda a: pl.BlockSpec((group,) + a.shape[1:], lambda s, pt: (s, 0, 0))
    grid_spec = pltpu.PrefetchScalarGridSpec(
        num_scalar_prefetch=1,
        grid=(nseq // group,),
        in_specs=[blk(qi_hs), blk(wi_hs), blk(kw_new), pl.BlockSpec(memory_space=pl.ANY)],
        out_specs=pl.BlockSpec((group, SUBLANES, past + LANES), lambda s, pt: (s, 0, 0)),
        scratch_shapes=[pltpu.VMEM((2, group * n_pages, IDX_DIM, page), F32),
                        pltpu.SemaphoreType.DMA((2,))],
    )
    return pl.pallas_call(
        functools.partial(_sample_index_kernel, n_pages=n_pages, page=page, steps=steps, k_sel=k_sel,
                          group=group),
        grid_spec=grid_spec,
        out_shape=jax.ShapeDtypeStruct((nseq, SUBLANES, past + LANES), F32),
        compiler_params=_cparams("arbitrary"),
        name="sample_index",
    )(page_table, qi_hs, wi_hs, kw_new, kidx_pages)


def _sample_attn_kernel(pt_ref, q_ref, bias_ref, biasn_ref, kn_ref, vn_ref, k_hbm, v_hbm, o_ref,
                        kbuf, vbuf, ksem, vsem, m_scr, l_scr, acc_scr,
                        *, n_chunks, cpages, page, n_heads):
    step = pl.program_id(0)
    nsteps = pl.num_programs(0)
    slot = step % 2

    def both(fn, item, sl):
        seq, ch = item // n_chunks, item % n_chunks
        fn(cpages, pt_ref, k_hbm, kbuf, ksem, seq, ch * cpages, sl, 0)
        fn(cpages, pt_ref, v_hbm, vbuf, vsem, seq, ch * cpages, sl, 0)

    @pl.when(step == 0)
    def _():
        both(_start_pages, step, 0)

    @pl.when(step + 1 < nsteps)
    def _():
        both(_start_pages, step + 1, 1 - slot)

    both(_wait_pages, step, slot)
    ch = step % n_chunks

    @pl.when(ch == 0)
    def _():
        m_scr[...] = jnp.full(m_scr.shape, NEG_INF, F32)
        l_scr[...] = jnp.zeros(l_scr.shape, F32)
        acc_scr[...] = jnp.zeros(acc_scr.shape, F32)

    nt = (((1,), (1,)), ((), ()))

    def update(k_t, v_t, bias8):
        sc = jnp.concatenate([jnp.dot(q_ref[0, h], k_t[h], preferred_element_type=F32)
                              for h in range(n_heads)], axis=0)
        sc = sc + jnp.concatenate([bias8] * n_heads, axis=0)
        m_old = m_scr[...]
        m_new = jnp.maximum(m_old, jnp.max(sc, axis=1, keepdims=True))
        m_safe = jnp.where(m_new == NEG_INF, 0.0, m_new)
        alpha = jnp.exp(m_old - m_safe)
        p = jnp.exp(sc - m_safe)
        l_scr[...] = alpha * l_scr[...] + jnp.sum(p, axis=1, keepdims=True)
        pv = jnp.concatenate([lax.dot_general(p[h * SUBLANES:(h + 1) * SUBLANES].astype(BF16), v_t[h], nt,
                                              preferred_element_type=F32)
                              for h in range(n_heads)], axis=0)
        acc_scr[...] = alpha * acc_scr[...] + pv
        m_scr[...] = m_new

    def chunk_t(buf, h):
        return jnp.concatenate([buf[slot, j, h].astype(BF16) for j in range(cpages)], axis=1)

    update([chunk_t(kbuf, h) for h in range(n_heads)], [chunk_t(vbuf, h) for h in range(n_heads)],
           bias_ref[0])

    @pl.when(ch == n_chunks - 1)
    def _():
        pad = jnp.zeros((LANES - SUBLANES, n_heads * HEAD_DIM), F32)

        def new_t(ref):
            x_t = jnp.concatenate([ref[0], pad], axis=0).T
            return [x_t[h * HEAD_DIM:(h + 1) * HEAD_DIM].astype(BF16) for h in range(n_heads)]

        update(new_t(kn_ref), new_t(vn_ref), biasn_ref[0])
        o = acc_scr[...] / l_scr[...]
        o_ref[0] = jnp.concatenate([o[h * SUBLANES:(h + 1) * SUBLANES] for h in range(n_heads)], axis=1)


def _sample_attention(page_table, q_hs, bias, k_new, v_new, k_pages, v_pages, cpages):
    nseq, n_pages = page_table.shape
    n_heads, page = k_pages.shape[1], k_pages.shape[3]
    aw = n_heads * HEAD_DIM
    n_chunks = n_pages // cpages
    cs = cpages * page
    rows = n_heads * SUBLANES
    grid_spec = pltpu.PrefetchScalarGridSpec(
        num_scalar_prefetch=1,
        grid=(nseq * n_chunks,),
        in_specs=[pl.BlockSpec((1, n_heads, SUBLANES, HEAD_DIM), lambda i, pt: (i // n_chunks, 0, 0, 0)),
                  pl.BlockSpec((1, SUBLANES, cs), lambda i, pt: (i // n_chunks, 0, i % n_chunks)),
                  pl.BlockSpec((1, SUBLANES, LANES), lambda i, pt: (i // n_chunks, 0, n_pages * page // LANES)),
                  pl.BlockSpec((1, SUBLANES, aw), lambda i, pt: (i // n_chunks, 0, 0)),
                  pl.BlockSpec((1, SUBLANES, aw), lambda i, pt: (i // n_chunks, 0, 0)),
                  pl.BlockSpec(memory_space=pl.ANY),
                  pl.BlockSpec(memory_space=pl.ANY)],
        out_specs=pl.BlockSpec((1, SUBLANES, aw), lambda i, pt: (i // n_chunks, 0, 0)),
        scratch_shapes=[pltpu.VMEM((2, cpages, n_heads, HEAD_DIM, page), F32),
                        pltpu.VMEM((2, cpages, n_heads, HEAD_DIM, page), F32),
                        pltpu.SemaphoreType.DMA((2,)),
                        pltpu.SemaphoreType.DMA((2,)),
                        pltpu.VMEM((rows, 1), F32),
                        pltpu.VMEM((rows, 1), F32),
                        pltpu.VMEM((rows, HEAD_DIM), F32)],
    )
    return pl.pallas_call(
        functools.partial(_sample_attn_kernel, n_chunks=n_chunks, cpages=cpages, page=page,
                          n_heads=n_heads),
        grid_spec=grid_spec,
        out_shape=jax.ShapeDtypeStruct((nseq, SUBLANES, aw), F32),
        compiler_params=_cparams("arbitrary"),
        name="sample_attn",
    )(page_table, q_hs, bias, bias, k_new, v_new, k_pages, v_pages)


def _pack_w_in(w, aw, sw):
    sizes = [aw] * 4 + [N_IDX_HEADS * IDX_DIM, IDX_DIM, N_IDX_HEADS, sw, sw]
    offs = np.cumsum([0] + sizes)
    seg = [w[:, offs[j]:offs[j + 1]] for j in range(len(sizes))]
    pad = jnp.zeros((w.shape[0], LANES - IDX_DIM - N_IDX_HEADS), w.dtype)
    return jnp.concatenate(seg[:5] + [seg[5], seg[6], pad] + seg[7:], axis=1).astype(BF16)


def _ssm_weights(a_re, a_im, log_dt, b_re, b_im, c_re, c_im, d_skip):
    g, n, c = b_re.shape
    a_re, a_im = a_re.astype(F32), a_im.astype(F32)
    dt = jnp.exp(log_dt.astype(F32))[:, None]
    mag = jnp.exp(dt * a_re)
    abr = mag * jnp.cos(dt * a_im)
    abi = mag * jnp.sin(dt * a_im)
    den = a_re * a_re + a_im * a_im
    nr = abr - 1.0
    cr = (nr * a_re + abi * a_im) / den
    ci = (abi * a_re - nr * a_im) / den
    b_re, b_im = b_re.astype(F32), b_im.astype(F32)
    bbr = cr[..., None] * b_re - ci[..., None] * b_im
    bbi = cr[..., None] * b_im + ci[..., None] * b_re

    kw_lanes = 2 * LANES
    gpk = kw_lanes // c
    n_in = g // gpk
    eye = jnp.eye(gpk, dtype=F32)

    def in_map(bb):
        bb = bb.reshape(n_in, gpk, n, c)
        return jnp.einsum("jgnc,gh->jgchn", bb, eye).reshape(n_in, gpk * c, gpk * n).astype(BF16)

    gps = LANES // c

    def out_map(cc):
        cc = cc.astype(F32).reshape(g // gps, gps, c, n)
        return jnp.einsum("sgcn,gh->sgnhc", cc, jnp.eye(gps, dtype=F32)).reshape(
            g // gps, gps * n, gps * c).astype(BF16)

    ar, ai = abr.reshape(1, g * n), abi.reshape(1, g * n)
    pows_r, pows_i = [ar], [ai]
    for _ in range(SUBLANES - 1):
        pr, pi = pows_r[-1], pows_i[-1]
        pows_r.append(pr * ar - pi * ai)
        pows_i.append(pr * ai + pi * ar)
    row = jnp.arange(SUBLANES)[:, None]

    def masked(k):
        keep = row >= k
        return (jnp.where(keep, pows_r[k - 1], 0.0), jnp.where(keep, pows_i[k - 1], 0.0))

    m1, m2, m4 = masked(1), masked(2), masked(4)
    tab = jnp.stack([m1[0], m1[1], m2[0], m2[1], m4[0], m4[1],
                     jnp.concatenate(pows_r, axis=0), jnp.concatenate(pows_i, axis=0)], axis=0)
    return (in_map(bbr), in_map(bbi), out_map(c_re), out_map(c_im),
            d_skip.astype(F32).reshape(1, g * c), tab)


def kernel(x_prompt, x_sample, cache_k, cache_v, cache_kidx, page_table, state_ssm_re, state_ssm_im,
           w_in, w_out, ssm_a_re, ssm_a_im, ssm_log_dt, ssm_b_re, ssm_b_im, ssm_c_re, ssm_c_im,
           ssm_d, w_glu, b_glu, ln_g, ln_b):
    depth = w_in.shape[0]
    assert depth == 1, "single-layer trunk"
    batch, t, d = x_prompt.shape
    nseq, steps, _ = x_sample.shape
    n_pool, page, n_heads, head_dim = cache_k.shape[1:]
    assert head_dim == HEAD_DIM and steps <= SUBLANES
    aw = n_heads * HEAD_DIM
    g_ssm_n = ssm_b_re.shape[1]
    sw = g_ssm_n * SSM_GROUP
    gn = g_ssm_n * SSM_STATE
    n_pages = page_table.shape[1]
    past = n_pages * page
    alpha = (2.0 * depth) ** 0.25

    layer = 0
    w_packed = _pack_w_in(w_in[layer], aw, sw)
    ssm_w = _ssm_weights(ssm_a_re[layer], ssm_a_im[layer], ssm_log_dt[layer], ssm_b_re[layer],
                         ssm_b_im[layer], ssm_c_re[layer], ssm_c_im[layer], ssm_d[layer])
    fin_w = (w_glu[layer].astype(BF16), b_glu[layer].astype(F32).reshape(1, sw),
             w_out[layer][:aw].astype(BF16), w_out[layer][aw:].astype(BF16),
             ln_g[layer].astype(F32).reshape(1, d), ln_b[layer].astype(F32).reshape(1, d))

    tm = 512
    xp2 = x_prompt.reshape(batch * t, d)
    q, k_t, v_t, g_att, qi, kw, ki_t, u, g_ssm = _project(
        xp2, w_packed, _rope_tables(jnp.arange(t)), tm, batch, aw, sw)
    attn = _prompt_attention(q, qi, kw, k_t, v_t, batch, t, n_heads, min(TOPK_MAX, t // 4))
    zeros_state = jnp.zeros((batch, 1, gn), F32)
    y_ssm, hr_p, hi_p = _ssm_prompt(u, zeros_state, zeros_state, ssm_w, batch, t, min(512, t))
    y_prompt = _finish(xp2, attn, g_att, y_ssm, g_ssm, fin_w, tm, alpha).reshape(batch, t, d)
    to_cache = lambda a_t: jnp.transpose(a_t.reshape(1, batch, n_heads, HEAD_DIM, t), (0, 1, 4, 2, 3))
    new_k_p = to_cache(k_t)
    new_v_p = to_cache(v_t)
    new_ki_p = jnp.transpose(ki_t, (0, 2, 1))[None]
    new_hr_p = hr_p.reshape(1, batch, g_ssm_n, SSM_STATE)
    new_hi_p = hi_p.reshape(1, batch, g_ssm_n, SSM_STATE)

    rows_s = nseq * steps
    xs2 = x_sample.reshape(rows_s, d)
    pos_s = past + (jnp.arange(rows_s) % steps)
    qs, ks_t, vs_t, gas, qis, kws, _, us, gss = _project(
        xs2, w_packed, _rope_tables(pos_s), rows_s, 1, aw, sw)
    ks, vs = ks_t[0].T, vs_t[0].T

    def head_rows(a, width):
        a = a.reshape(nseq, steps, -1, width).transpose(0, 2, 1, 3)
        return jnp.pad(a, ((0, 0), (0, 0), (0, SUBLANES - steps), (0, 0)))

    def step_rows(a):
        return jnp.pad(a.reshape(nseq, steps, -1), ((0, 0), (0, SUBLANES - steps), (0, 0)))

    qi_hs = head_rows(qis.T, IDX_DIM).reshape(nseq, N_IDX_HEADS * SUBLANES, IDX_DIM)
    wi_hs = head_rows(kws[:, IDX_DIM:IDX_DIM + N_IDX_HEADS], 1).reshape(nseq, N_IDX_HEADS * SUBLANES, 1)
    q_hs = head_rows(qs.T, HEAD_DIM)

    kidx_t = jnp.transpose(cache_kidx[layer], (0, 2, 1))
    ck_t = jnp.transpose(cache_k[layer], (0, 2, 3, 1))
    cv_t = jnp.transpose(cache_v[layer], (0, 2, 3, 1))

    k_sel_s = min(TOPK_MAX, (past + steps) // 4)
    group = 4 if nseq % 4 == 0 else 1
    bias = _sample_index(page_table, qi_hs, wi_hs, step_rows(kws), kidx_t, steps, k_sel_s, group)
    cpages = 16 if n_pages % 16 == 0 else n_pages
    attn_s = _sample_attention(page_table, q_hs, bias, step_rows(ks), step_rows(vs), ck_t, cv_t, cpages)
    attn_s = attn_s[:, :steps].reshape(rows_s, aw)

    y_s, hr_s, hi_s = _ssm_sample(us.reshape(nseq, steps * sw),
                                  state_ssm_re[layer].reshape(nseq, gn).astype(F32),
                                  state_ssm_im[layer].reshape(nseq, gn).astype(F32), ssm_w, steps)
    y_sample = _finish(xs2, attn_s, gas, y_s.reshape(rows_s, sw), gss, fin_w, rows_s, alpha)
    y_sample = y_sample.reshape(nseq, steps, d)

    return (y_prompt, y_sample, new_k_p, new_v_p, new_ki_p, new_hr_p, new_hi_p,
            ks.reshape(1, nseq, steps, n_heads, HEAD_DIM), vs.reshape(1, nseq, steps, n_heads, HEAD_DIM),
            kws[:, :IDX_DIM].reshape(1, nseq, steps, IDX_DIM),
            hr_s.reshape(1, nseq, g_ssm_n, SSM_STATE), hi_s.reshape(1, nseq, g_ssm_n, SSM_STATE))
```
